```python
import math
import jax
import jax.numpy as jnp
from jax import lax
import numpy as np

D_MODEL = 1024
BATCH = 4
SEQ = 4096
DEPTH = 1
DEC_BATCH = 128
DEC_SEQ = 8
PAST_LEN = 2048
PAGE_SIZE = 128

H_A = 8
D_A = D_MODEL // (2 * H_A)
H_B = 8
D_B = D_MODEL // H_B
W_A = H_A * 2 * D_A
W_B = H_B * D_B
MOBA_BLOCK = 256
MOBA_TOPK = 3
MOBA_Q_CHUNK = 64
ATTN_Q_BLOCK = 128
D_FF = 4 * D_MODEL
D_PLE = 256
ROPE_THETA = 10000.0
EPS = 1e-6
N_IN = 3 * W_A + 3 * W_B + 2 * D_MODEL

kernel_name = 'diffattn_moba_hybrid_step'


def _rmsnorm(x, g):
    xf = x.astype(jnp.float32)
    y = xf * lax.rsqrt(jnp.mean(xf * xf, axis=-1, keepdims=True) + EPS)
    return (y * g.astype(jnp.float32)).astype(x.dtype)


def _rope(x, pos):
    d = x.shape[-1]
    half = d // 2
    inv = ROPE_THETA ** (-jnp.arange(half, dtype=jnp.float32) * 2.0 / d)
    ang = pos.astype(jnp.float32)[:, None] * inv[None, :]
    cos = jnp.cos(ang)[:, None, :]
    sin = jnp.sin(ang)[:, None, :]
    xf = x.astype(jnp.float32)
    x1, x2 = xf[..., :half], xf[..., half:]
    return jnp.concatenate([x1 * cos - x2 * sin, x2 * cos + x1 * sin], axis=-1).astype(x.dtype)


def _qk(t, H, d, g, pos):
    B, S = t.shape[0], t.shape[1]
    return _rope(_rmsnorm(t.reshape(B, S, H, d), g), pos)


def _project(x, pos, g_mix, w_in, qn_a, kn_a, qn_b, kn_b):
    B, S, _ = x.shape
    z = _rmsnorm(x, g_mix) @ w_in
    offs = [W_A, 2 * W_A, 3 * W_A, 3 * W_A + W_B, 3 * W_A + 2 * W_B, 3 * W_A + 3 * W_B, 3 * W_A + 3 * W_B + D_MODEL]
    qa, ka, va, qb, kb, vb, ga, gb = jnp.split(z, offs, axis=-1)
    qa = _qk(qa, 2 * H_A, D_A, qn_a, pos).reshape(B, S, H_A, 2 * D_A)
    ka = _qk(ka, 2 * H_A, D_A, kn_a, pos).reshape(B, S, H_A, 2 * D_A)
    va = va.reshape(B, S, H_A, 2 * D_A)
    qb = _qk(qb, H_B, D_B, qn_b, pos)
    kb = _qk(kb, H_B, D_B, kn_b, pos)
    vb = vb.reshape(B, S, H_B, D_B)
    return qa, ka, va, qb, kb, vb, ga, gb


def _diff_attend(q, k, v, q_pos, k_pos, lam, lam_init, g_sub):
    scale = D_A ** -0.5
    mask = k_pos[None, :] <= q_pos[:, None]

    def amap(qi, ki):
        s = jnp.einsum('bqhd,bkhd->bhqk', qi, ki).astype(jnp.float32) * scale
        return jax.nn.softmax(jnp.where(mask, s, -jnp.inf), axis=-1)

    a = amap(q[..., :D_A], k[..., :D_A]) - lam * amap(q[..., D_A:], k[..., D_A:])
    o = jnp.einsum('bhqk,bkhe->bqhe', a.astype(v.dtype), v)
    o = _rmsnorm(o, g_sub) * (1.0 - lam_init)
    return o.reshape(o.shape[0], o.shape[1], W_A)


def _moba_attend(q, k, v, q_pos, chunk):
    B, Q, H, dh = q.shape
    L = k.shape[1]
    nb = -(-L // MOBA_BLOCK)
    pad = nb * MOBA_BLOCK - L

    def blocks(t):
        t = jnp.pad(t, ((0, 0), (0, pad), (0, 0), (0, 0)))
        return t.reshape(B, nb, MOBA_BLOCK, H, dh).transpose(0, 3, 1, 2, 4)

    kblk, vblk = blocks(k), blocks(v)
    kmean = jnp.mean(kblk.astype(jnp.float32), axis=3)
    topk = min(MOBA_TOPK, nb)
    bi = jnp.arange(B)[:, None, None, None]
    hi = jnp.arange(H)[None, :, None, None]
    r = jnp.arange(MOBA_BLOCK)
    scale = dh ** -0.5

    def one(args):
        qc, pc = args
        C = pc.shape[0]
        own = pc // MOBA_BLOCK
        g = jnp.einsum('bqhd,bhnd->bhqn', qc.astype(jnp.float32), kmean)
        g = jnp.where(jnp.arange(nb)[None, :] < own[:, None], g, -jnp.inf)
        _, sel = lax.top_k(g, topk)
        own_b = jnp.broadcast_to(own[None, None, :, None], (B, H, C, 1))
        idx = jnp.concatenate([sel, own_b], axis=-1)
        slot_ok = jnp.concatenate([sel < own_b, jnp.ones((B, H, C, 1), dtype=bool)], axis=-1)
        kp = idx[..., None] * MOBA_BLOCK + r
        valid = slot_ok[..., None] & (kp <= pc[None, None, :, None, None])
        kg = kblk[bi, hi, idx]
        vg = vblk[bi, hi, idx]
        s = jnp.einsum('bqhd,bhqnkd->bhqnk', qc, kg).astype(jnp.float32) * scale
        s = jnp.where(valid, s, -jnp.inf)
        a = jax.nn.softmax(s.reshape(B, H, C, -1), axis=-1).reshape(s.shape)
        return jnp.einsum('bhqnk,bhqnkd->bqhd', a.astype(vg.dtype), vg)

    n_chunks = Q // chunk
    qs = q.reshape(B, n_chunks, chunk, H, dh).transpose(1, 0, 2, 3, 4)
    ps = q_pos.reshape(n_chunks, chunk)
    o = lax.map(one, (qs, ps))
    return o.transpose(1, 0, 2, 3, 4).reshape(B, Q, H * dh)


def _post(x, oa, ob, ga, gb, pe, w_br_a, w_br_b, w_o, g_mlp, w_up, w_down, g_ple, w_ple_gate, w_ple_proj):
    merged = jax.nn.sigmoid(ga) * (oa @ w_br_a) + jax.nn.sigmoid(gb) * (ob @ w_br_b)
    h = x + merged @ w_o
    u = _rmsnorm(h, g_mlp) @ w_up
    h = h + jnp.square(jax.nn.relu(u)) @ w_down
    gate = jax.nn.sigmoid(_rmsnorm(h, g_ple) @ w_ple_gate)
    return h + gate * (pe @ w_ple_proj)


def setup_inputs(seed: int = 0) -> dict:
    key = jax.random.key(seed)
    ks = jax.random.split(key, 32)
    n_pages = PAST_LEN // PAGE_SIZE
    in_use = DEC_BATCH * n_pages
    n_pool = in_use + max(1, in_use // 4)
    nrm = jax.random.normal
    f32 = jnp.float32

    def gain(k, d):
        return 1.0 + 0.02 * nrm(k, (DEPTH, d), f32)

    perm = jax.random.permutation(ks[0], n_pool)
    page_table = perm[:in_use].reshape(DEC_BATCH, n_pages).astype(jnp.int32)
    return {
        'x_prompt': nrm(ks[1], (BATCH, SEQ, D_MODEL), f32),
        'x_sample': nrm(ks[2], (DEC_BATCH, DEC_SEQ, D_MODEL), f32),
        'p_prompt': nrm(ks[3], (DEPTH, BATCH, SEQ, D_PLE), f32),
        'p_sample': nrm(ks[4], (DEPTH, DEC_BATCH, DEC_SEQ, D_PLE), f32),
        'cache_a_k': nrm(ks[5], (DEPTH, n_pool, PAGE_SIZE, H_A, 2 * D_A), f32),
        'cache_a_v': nrm(ks[6], (DEPTH, n_pool, PAGE_SIZE, H_A, 2 * D_A), f32),
        'cache_b_k': nrm(ks[7], (DEPTH, n_pool, PAGE_SIZE, H_B, D_B), f32),
        'cache_b_v': nrm(ks[8], (DEPTH, n_pool, PAGE_SIZE, H_B, D_B), f32),
        'page_table': page_table,
        'g_mix': gain(ks[9], D_MODEL),
        'w_in': nrm(ks[10], (DEPTH, D_MODEL, N_IN), f32) * D_MODEL ** -0.5,
        'qn_a': gain(ks[11], D_A),
        'kn_a': gain(ks[12], D_A),
        'lam_q1': 0.1 * nrm(ks[13], (DEPTH, D_A), f32),
        'lam_k1': 0.1 * nrm(ks[14], (DEPTH, D_A), f32),
        'lam_q2': 0.1 * nrm(ks[15], (DEPTH, D_A), f32),
        'lam_k2': 0.1 * nrm(ks[16], (DEPTH, D_A), f32),
        'g_sub_a': gain(ks[17], 2 * D_A),
        'qn_b': gain(ks[18], D_B),
        'kn_b': gain(ks[19], D_B),
        'w_br_a': nrm(ks[20], (DEPTH, W_A, D_MODEL), f32) * W_A ** -0.5,
        'w_br_b': nrm(ks[21], (DEPTH, W_B, D_MODEL), f32) * W_B ** -0.5,
        'w_o': nrm(ks[22], (DEPTH, D_MODEL, D_MODEL), f32) * D_MODEL ** -0.5,
        'g_mlp': gain(ks[23], D_MODEL),
        'w_up': nrm(ks[24], (DEPTH, D_MODEL, D_FF), f32) * D_MODEL ** -0.5,
        'w_down': nrm(ks[25], (DEPTH, D_FF, D_MODEL), f32) * D_FF ** -0.5,
        'g_ple': gain(ks[26], D_MODEL),
        'w_ple_gate': nrm(ks[27], (DEPTH, D_MODEL, D_MODEL), f32) * D_MODEL ** -0.5,
        'w_ple_proj': nrm(ks[28], (DEPTH, D_PLE, D_MODEL), f32) * D_PLE ** -0.5,
    }


def reference(x_prompt, x_sample, p_prompt, p_sample, cache_a_k, cache_a_v, cache_b_k, cache_b_v, page_table,
              g_mix, w_in, qn_a, kn_a, lam_q1, lam_k1, lam_q2, lam_k2, g_sub_a, qn_b, kn_b,
              w_br_a, w_br_b, w_o, g_mlp, w_up, w_down, g_ple, w_ple_gate, w_ple_proj):
    S = x_prompt.shape[1]
    DB, T = x_sample.shape[0], x_sample.shape[1]
    past = page_table.shape[1] * cache_a_k.shape[2]
    pos_p = jnp.arange(S, dtype=jnp.int32)
    pos_k = jnp.arange(past + T, dtype=jnp.int32)
    pos_s = pos_k[past:]
    hp, hs = x_prompt, x_sample
    kap, vap, kbp, vbp, kas, vas, kbs, vbs = [], [], [], [], [], [], [], []
    for l in range(DEPTH):
        lam_init = 0.8 - 0.6 * math.exp(-0.3 * l)
        lam = (jnp.exp(jnp.sum(lam_q1[l].astype(jnp.float32) * lam_k1[l].astype(jnp.float32)))
               - jnp.exp(jnp.sum(lam_q2[l].astype(jnp.float32) * lam_k2[l].astype(jnp.float32))) + lam_init)
        post_w = (w_br_a[l], w_br_b[l], w_o[l], g_mlp[l], w_up[l], w_down[l], g_ple[l], w_ple_gate[l], w_ple_proj[l])

        qa, ka, va, qb, kb, vb, ga, gb = _project(hp, pos_p, g_mix[l], w_in[l], qn_a[l], kn_a[l], qn_b[l], kn_b[l])
        outs = []
        for i in range(S // ATTN_Q_BLOCK):
            s0, e = i * ATTN_Q_BLOCK, (i + 1) * ATTN_Q_BLOCK
            outs.append(_diff_attend(qa[:, s0:e], ka[:, :e], va[:, :e], pos_p[s0:e], pos_p[:e], lam, lam_init, g_sub_a[l]))
        oa = jnp.concatenate(outs, axis=1)
        ob = _moba_attend(qb, kb, vb, pos_p, math.gcd(S, MOBA_Q_CHUNK))
        hp = _post(hp, oa, ob, ga, gb, p_prompt[l], *post_w)
        kap.append(ka); vap.append(va); kbp.append(kb); vbp.append(vb)

        qa_s, ka_s, va_s, qb_s, kb_s, vb_s, ga_s, gb_s = _project(hs, pos_s, g_mix[l], w_in[l], qn_a[l], kn_a[l], qn_b[l], kn_b[l])
        ka_all = jnp.concatenate([cache_a_k[l][page_table].reshape(DB, past, H_A, 2 * D_A).astype(ka_s.dtype), ka_s], axis=1)
        va_all = jnp.concatenate([cache_a_v[l][page_table].reshape(DB, past, H_A, 2 * D_A).astype(va_s.dtype), va_s], axis=1)
        kb_all = jnp.concatenate([cache_b_k[l][page_table].reshape(DB, past, H_B, D_B).astype(kb_s.dtype), kb_s], axis=1)
        vb_all = jnp.concatenate([cache_b_v[l][page_table].reshape(DB, past, H_B, D_B).astype(vb_s.dtype), vb_s], axis=1)
        oa_s = _diff_attend(qa_s, ka_all, va_all, pos_s, pos_k, lam, lam_init, g_sub_a[l])
        ob_s = _moba_attend(qb_s, kb_all, vb_all, pos_s, 1)
        hs = _post(hs, oa_s, ob_s, ga_s, gb_s, p_sample[l], *post_w)
        kas.append(ka_s); vas.append(va_s); kbs.append(kb_s); vbs.append(vb_s)

    return (hp, hs, jnp.stack(kap), jnp.stack(vap), jnp.stack(kbp), jnp.stack(vbp),
            jnp.stack(kas), jnp.stack(vas), jnp.stack(kbs), jnp.stack(vbs))
```

```python
import functools
import math

import jax
import jax.numpy as jnp
from jax import lax
from jax.experimental import pallas as pl
from jax.experimental.pallas import tpu as pltpu

F32 = jnp.float32
BF16 = jnp.bfloat16

LANES = 128
H_A = 8
D_A = 64
H_B = 8
D_B = 128
MOBA_BLOCK = 256
MOBA_TOPK = 3
ROPE_THETA = 10000.0
EPS = 1e-6
N_SEG = 8
NEG_INF = float("-inf")
VMEM_LIMIT = 56 * 1024 * 1024
ATT_TILE = 256


def _cparams(*sem):
    return pltpu.CompilerParams(dimension_semantics=sem, vmem_limit_bytes=VMEM_LIMIT)


def _rms(x, g):
    return x * lax.rsqrt(jnp.mean(x * x, axis=-1, keepdims=True) + EPS) * g


def _xn_kernel(x_ref, g_ref, o_ref):
    o_ref[...] = _rms(x_ref[...], g_ref[...]).astype(o_ref.dtype)


def _input_norm(x, g, tm):
    m, d = x.shape
    return pl.pallas_call(
        _xn_kernel,
        grid=(m // tm,),
        in_specs=[pl.BlockSpec((tm, d), lambda i: (i, 0)), pl.BlockSpec((1, d), lambda i: (0, 0))],
        out_specs=pl.BlockSpec((tm, d), lambda i: (i, 0)),
        out_shape=jax.ShapeDtypeStruct((m, d), BF16),
        compiler_params=_cparams("parallel"),
        name="input_norm",
    )(x, g)


def _norm_rope(zh, g, c, s, group):
    sq = zh * zh
    lane = lax.broadcasted_iota(jnp.int32, zh.shape, 1)
    if group == LANES:
        ss = jnp.sum(sq, axis=-1, keepdims=True)
    else:
        lo = lane < group
        s_lo = jnp.sum(jnp.where(lo, sq, 0.0), axis=-1, keepdims=True)
        s_all = jnp.sum(sq, axis=-1, keepdims=True)
        ss = jnp.where(lo, s_lo, s_all - s_lo)
    y = zh * lax.rsqrt(ss * (1.0 / group) + EPS) * g
    half = group // 2
    if group == LANES:
        sw = pltpu.roll(y, half, axis=1)
    else:
        first = (lane % group) < half
        sw = jnp.where(first, pltpu.roll(y, LANES - half, axis=1), pltpu.roll(y, half, axis=1))
    return y * c + sw * s


def _proj_plain_kernel(xn_ref, w_ref, o_ref):
    o_ref[...] = jnp.dot(xn_ref[...], w_ref[...], preferred_element_type=F32).astype(o_ref.dtype)


def _proj_rope_kernel(group, xn_ref, w_ref, g_ref, c_ref, s_ref, o_ref):
    z = jnp.dot(xn_ref[...], w_ref[...], preferred_element_type=F32)
    g, c, s = g_ref[...], c_ref[...], s_ref[...]
    for h in range(z.shape[1] // LANES):
        cols = slice(h * LANES, (h + 1) * LANES)
        o_ref[:, cols] = _norm_rope(z[:, cols], g, c, s, group).astype(o_ref.dtype)


def _project(xn, w, seg, tm, out_dtype, rope=None):
    m, d = xn.shape
    wseg = w.shape[1] // N_SEG
    xn_spec = pl.BlockSpec((tm, d), lambda i: (i, 0))
    w_spec = pl.BlockSpec((d, wseg), lambda i: (0, seg))
    out_spec = pl.BlockSpec((tm, wseg), lambda i: (i, 0))
    out_shape = jax.ShapeDtypeStruct((m, wseg), out_dtype)
    if rope is None:
        return pl.pallas_call(
            _proj_plain_kernel, grid=(m // tm,), in_specs=[xn_spec, w_spec], out_specs=out_spec,
            out_shape=out_shape, compiler_params=_cparams("parallel"), name=f"proj_plain_{seg}",
        )(xn, w)
    group, g, c, s = rope
    n_tab = c.shape[0] // tm
    tab_spec = pl.BlockSpec((tm, LANES), lambda i: (i % n_tab, 0))
    return pl.pallas_call(
        functools.partial(_proj_rope_kernel, group), grid=(m // tm,),
        in_specs=[xn_spec, w_spec, pl.BlockSpec((1, LANES), lambda i: (0, 0)), tab_spec, tab_spec],
        out_specs=out_spec, out_shape=out_shape, compiler_params=_cparams("parallel"),
        name=f"proj_rope_{seg}",
    )(xn, w, g, c, s)


def _rope_tables(pos, d):
    half = d // 2
    inv = ROPE_THETA ** (-jnp.arange(half, dtype=F32) * 2.0 / d)
    ang = pos.astype(F32)[:, None] * inv[None, :]
    cos, sin = jnp.cos(ang), jnp.sin(ang)
    reps = LANES // d
    c = jnp.tile(jnp.concatenate([cos, cos], axis=-1), (1, reps))
    s = jnp.tile(jnp.concatenate([-sin, sin], axis=-1), (1, reps))
    return c, s


def _stage_transposed(src_ref, dst_scr, n_tiles, row_mask=None):
    for i in range(n_tiles):
        t = src_ref[i * ATT_TILE:(i + 1) * ATT_TILE, :].astype(F32).T
        if row_mask is not None:
            t = jnp.where(row_mask, t, 0.0)
        dst_scr[i] = t.astype(BF16)


def _softmax_step(s, m, l):
    m_new = jnp.maximum(m, jnp.max(s, axis=0, keepdims=True))
    p = jnp.exp(s - m_new)
    alpha = jnp.exp(m - m_new)
    return p, alpha, m_new, alpha * l + jnp.sum(p, axis=0, keepdims=True)


def _diff_attn_kernel(lam_ref, q_ref, k_ref, v_ref, gsub_ref, o_ref,
                      q1t_scr, q2t_scr, k_scr, vt_scr, acc1_scr, acc2_scr, *, out_scale):
    t = ATT_TILE
    n_tiles = q_ref.shape[0] // t
    row = lax.broadcasted_iota(jnp.int32, (LANES, t), 0)
    _stage_transposed(q_ref, q1t_scr, n_tiles, row < D_A)
    _stage_transposed(q_ref, q2t_scr, n_tiles, row >= D_A)
    _stage_transposed(v_ref, vt_scr, n_tiles)
    k_scr[...] = k_ref[...].astype(BF16)
    lam = lam_ref[0, 0]
    causal = (lax.broadcasted_iota(jnp.int32, (t, t), 0) <= lax.broadcasted_iota(jnp.int32, (t, t), 1))

    def q_tile(qi, carry):
        q1t, q2t = q1t_scr[qi], q2t_scr[qi]

        def kv_block(j, stats, mask):
            m1, l1, m2, l2 = stats
            kj = k_scr[pl.ds(pl.multiple_of(j * t, t), t), :]
            vtj = vt_scr[j]
            s1 = jnp.dot(kj, q1t, preferred_element_type=F32)
            s2 = jnp.dot(kj, q2t, preferred_element_type=F32)
            if mask is not None:
                s1 = jnp.where(mask, s1, NEG_INF)
                s2 = jnp.where(mask, s2, NEG_INF)
            p1, a1, m1, l1 = _softmax_step(s1, m1, l1)
            p2, a2, m2, l2 = _softmax_step(s2, m2, l2)
            acc1_scr[...] = acc1_scr[...] * a1 + jnp.dot(vtj, p1.astype(BF16), preferred_element_type=F32)
            acc2_scr[...] = acc2_scr[...] * a2 + jnp.dot(vtj, p2.astype(BF16), preferred_element_type=F32)
            return m1, l1, m2, l2

        acc1_scr[...] = jnp.zeros_like(acc1_scr)
        acc2_scr[...] = jnp.zeros_like(acc2_scr)
        neg = jnp.full((1, t), NEG_INF, F32)
        zero = jnp.zeros((1, t), F32)
        stats = kv_block(qi, (neg, zero, neg, zero), causal)
        m1, l1, m2, l2 = lax.fori_loop(0, qi, lambda j, st: kv_block(j, st, None), stats)
        ot = acc1_scr[...] / l1 - lam * (acc2_scr[...] / l2)
        o = _rms(ot.T, gsub_ref[...]) * out_scale
        o_ref[pl.ds(pl.multiple_of(qi * t, t), t), :] = o.astype(o_ref.dtype)
        return carry

    lax.fori_loop(0, n_tiles, q_tile, 0)


def _diff_attention(lam, q, k, v, g_sub, out_scale):
    b, s, w = q.shape
    n_tiles = s // ATT_TILE
    blk = lambda: pl.BlockSpec((None, s, LANES), lambda bi, h: (bi, 0, h))
    return pl.pallas_call(
        functools.partial(_diff_attn_kernel, out_scale=out_scale),
        grid=(b, w // LANES),
        in_specs=[pl.BlockSpec(memory_space=pltpu.SMEM), blk(), blk(), blk(),
                  pl.BlockSpec((1, LANES), lambda bi, h: (0, 0))],
        out_specs=blk(),
        out_shape=jax.ShapeDtypeStruct((b, s, w), BF16),
        scratch_shapes=[pltpu.VMEM((n_tiles, LANES, ATT_TILE), BF16), pltpu.VMEM((n_tiles, LANES, ATT_TILE), BF16),
                        pltpu.VMEM((s, LANES), BF16), pltpu.VMEM((n_tiles, LANES, ATT_TILE), BF16),
                        pltpu.VMEM((LANES, ATT_TILE), F32), pltpu.VMEM((LANES, ATT_TILE), F32)],
        compiler_params=_cparams("parallel", "parallel"),
        name="diff_attn",
    )(lam, q, k, v, g_sub)


def _split_hi_lo(x):
    hi = x.astype(BF16)
    return hi, (x - hi.astype(F32)).astype(BF16)


def _topk_bias(g, n_valid):
    nb = g.shape[0]
    idx = lax.broadcasted_iota(jnp.int32, g.shape, 0)
    rank = jnp.zeros(g.shape, jnp.int32)
    for m in range(nb):
        gm = g[m:m + 1, :]
        beats = jnp.where(gm > g, 1, jnp.where((gm == g) & (idx > m), 1, 0))
        rank = rank + jnp.where(n_valid > m, beats, 0)
    sel = (idx < n_valid) & (rank < MOBA_TOPK)
    return jnp.where(sel, 0.0, NEG_INF)


def _moba_kernel(q_ref, k_ref, v_ref, o_ref, qt_scr, k_scr, vt_scr, bias_scr, acc_scr):
    t = ATT_TILE
    n_tiles = q_ref.shape[0] // t
    _stage_transposed(q_ref, qt_scr, n_tiles)
    _stage_transposed(v_ref, vt_scr, n_tiles)
    k = k_ref[...]
    k_scr[...] = k.astype(BF16)
    km_hi, km_lo = _split_hi_lo(jnp.mean(k.reshape(n_tiles, t, LANES), axis=1))
    causal = (lax.broadcasted_iota(jnp.int32, (t, t), 0) <= lax.broadcasted_iota(jnp.int32, (t, t), 1))

    def q_tile(qi, carry):
        qt = qt_scr[qi]
        gate = jnp.dot(km_hi, qt, preferred_element_type=F32) + jnp.dot(km_lo, qt, preferred_element_type=F32)
        bias = _topk_bias(gate, qi)
        for n in range(n_tiles):
            bias_scr[n] = bias[n:n + 1, :]

        def kv_block(j, stats, diagonal):
            m, l = stats
            kj = k_scr[pl.ds(pl.multiple_of(j * t, t), t), :]
            s = jnp.dot(kj, qt, preferred_element_type=F32)
            s = jnp.where(causal, s, NEG_INF) if diagonal else s + bias_scr[j]
            p, a, m, l = _softmax_step(s, m, l)
            acc_scr[...] = acc_scr[...] * a + jnp.dot(vt_scr[j], p.astype(BF16), preferred_element_type=F32)
            return m, l

        acc_scr[...] = jnp.zeros_like(acc_scr)
        stats = kv_block(qi, (jnp.full((1, t), NEG_INF, F32), jnp.zeros((1, t), F32)), True)
        m, l = lax.fori_loop(0, qi, lambda j, st: kv_block(j, st, False), stats)
        o_ref[pl.ds(pl.multiple_of(qi * t, t), t), :] = (acc_scr[...] / l).T.astype(o_ref.dtype)
        return carry

    lax.fori_loop(0, n_tiles, q_tile, 0)


def _moba_attention(q, k, v):
    b, s, w = q.shape
    n_tiles = s // ATT_TILE
    blk = lambda: pl.BlockSpec((None, s, LANES), lambda bi, h: (bi, 0, h))
    return pl.pallas_call(
        _moba_kernel,
        grid=(b, w // LANES),
        in_specs=[blk(), blk(), blk()],
        out_specs=blk(),
        out_shape=jax.ShapeDtypeStruct((b, s, w), BF16),
        scratch_shapes=[pltpu.VMEM((n_tiles, LANES, ATT_TILE), BF16), pltpu.VMEM((s, LANES), BF16),
                        pltpu.VMEM((n_tiles, LANES, ATT_TILE), BF16), pltpu.VMEM((n_tiles, 1, ATT_TILE), F32),
                        pltpu.VMEM((LANES, ATT_TILE), F32)],
        compiler_params=_cparams("parallel", "parallel"),
        name="moba_attn",
    )(q, k, v)


def _block_diag_queries(q, n_rep):
    t, w = q.shape
    used = n_rep * (w // LANES) * t
    rows = -(-used // LANES) * LANES
    tiled = jnp.concatenate([q] * (rows // t), axis=0)
    r = lax.broadcasted_iota(jnp.int32, (rows, w), 0)
    c = lax.broadcasted_iota(jnp.int32, (rows, w), 1)
    head = (r // t) % (w // LANES)
    part = r // (t * (w // LANES))
    keep = (c // LANES == head) & ((c % LANES) // (LANES // n_rep) == part) & (r < used)
    return jnp.where(keep, tiled, 0.0).astype(BF16)


def _dot_nt(a, b):
    return lax.dot_general(a, b, (((1,), (1,)), ((), ())), preferred_element_type=F32)


def _dot_tn(a, b):
    return lax.dot_general(a, b, (((0,), (0,)), ((), ())), preferred_element_type=F32)


def _head_diagonal(x, t):
    return jnp.concatenate(
        [x[h * t:(h + 1) * t, h * LANES:(h + 1) * LANES] for h in range(x.shape[1] // LANES)], axis=1)


def _load_page(ref):
    n_head = H_A
    slots = ref.shape[0] // n_head
    return jnp.concatenate([ref[pl.ds(h, slots, stride=n_head), :] for h in range(n_head)], axis=1)


def _new_token_scores(kn_ref, qbd, tq):
    s_new = _dot_nt(kn_ref[...].astype(BF16), qbd)
    key_t = lax.broadcasted_iota(jnp.int32, s_new.shape, 0)
    qry_t = lax.broadcasted_iota(jnp.int32, s_new.shape, 1) % tq
    return jnp.where(key_t <= qry_t, s_new, NEG_INF)


def _softmax_times_values(s_scr, s_new, m, vn_ref, v_pages):
    page = v_pages[0].shape[0] // H_A
    p_new = jnp.exp(s_new - m)
    l = jnp.sum(p_new, axis=0, keepdims=True)
    for p in range(len(v_pages)):
        rows = slice(p * page, (p + 1) * page)
        pp = jnp.exp(s_scr[rows, :] - m)
        s_scr[rows, :] = pp
        l = l + jnp.sum(pp, axis=0, keepdims=True)
    inv = 1.0 / l
    acc = _dot_tn((p_new * inv).astype(BF16), vn_ref[...].astype(BF16))
    for p in range(len(v_pages)):
        pp = s_scr[p * page:(p + 1) * page, :] * inv
        acc = acc + _dot_tn(pp.astype(BF16), _load_page(v_pages[p]).astype(BF16))
    return acc


def _sample_diff_kernel(n_pages, out_scale, pt_ref, lam_ref, q_ref, kn_ref, vn_ref, gsub_ref, *rest):
    k_pages = rest[:n_pages]
    v_pages = rest[n_pages:2 * n_pages]
    o_ref, s_scr = rest[2 * n_pages], rest[2 * n_pages + 1]
    tq = q_ref.shape[0]
    page = k_pages[0].shape[0] // H_A
    qbd = _block_diag_queries(q_ref[...].astype(F32), 2)
    s_new = _new_token_scores(kn_ref, qbd, tq)
    m = jnp.max(s_new, axis=0, keepdims=True)
    for p in range(n_pages):
        sp = _dot_nt(_load_page(k_pages[p]).astype(BF16), qbd)
        s_scr[p * page:(p + 1) * page, :] = sp
        m = jnp.maximum(m, jnp.max(sp, axis=0, keepdims=True))
    acc = _softmax_times_values(s_scr, s_new, m, vn_ref, v_pages)
    half = (q_ref.shape[1] // LANES) * tq
    o = _head_diagonal(acc[:half] - lam_ref[0, 0] * acc[half:2 * half], tq)
    g = gsub_ref[...]
    for h in range(o.shape[1] // LANES):
        cols = slice(h * LANES, (h + 1) * LANES)
        o_ref[:, cols] = (_rms(o[:, cols], g) * out_scale).astype(o_ref.dtype)


def _sample_moba_kernel(n_pages, pt_ref, q_ref, kn_ref, vn_ref, *rest):
    k_pages = rest[:n_pages]
    v_pages = rest[n_pages:2 * n_pages]
    o_ref, s_scr = rest[2 * n_pages], rest[2 * n_pages + 1]
    tq = q_ref.shape[0]
    page = k_pages[0].shape[0] // H_B
    per_blk = MOBA_BLOCK // page
    n_blk = n_pages // per_blk
    qbd = _block_diag_queries(q_ref[...].astype(F32), 1)
    means = []
    for n in range(n_blk):
        tot = jnp.zeros((1, q_ref.shape[1]), F32)
        for p in range(n * per_blk, (n + 1) * per_blk):
            kp = _load_page(k_pages[p])
            tot = tot + jnp.sum(kp, axis=0, keepdims=True)
            s_scr[p * page:(p + 1) * page, :] = _dot_nt(kp.astype(BF16), qbd)
        means.append(tot * (1.0 / MOBA_BLOCK))
    km_hi, km_lo = _split_hi_lo(jnp.concatenate(means, axis=0))
    bias = _topk_bias(_dot_nt(km_hi, qbd) + _dot_nt(km_lo, qbd), n_blk)
    s_new = _new_token_scores(kn_ref, qbd, tq)
    m = jnp.max(s_new, axis=0, keepdims=True)
    for p in range(n_pages):
        rows = slice(p * page, (p + 1) * page)
        sp = s_scr[rows, :] + bias[p // per_blk:p // per_blk + 1, :]
        s_scr[rows, :] = sp
        m = jnp.maximum(m, jnp.max(sp, axis=0, keepdims=True))
    acc = _softmax_times_values(s_scr, s_new, m, vn_ref, v_pages)
    o_ref[...] = _head_diagonal(acc, tq).astype(o_ref.dtype)


def _sample_attention(kernel_fn, name, page_table, scalars, q, k_new, v_new, extra, cache_k, cache_v, n_col):
    db, tq, w = q.shape
    n_pages = page_table.shape[1]
    page_rows = cache_k.shape[1]
    page = page_rows // (w // LANES)
    tok = lambda: pl.BlockSpec((None, tq, w), lambda b, pt: (b, 0, 0))
    page_spec = lambda p: pl.BlockSpec((None, page_rows, LANES), lambda b, pt: (pt[b * n_pages + p], 0, 0))
    in_specs = ([pl.BlockSpec(memory_space=pltpu.SMEM) for _ in scalars] + [tok(), tok(), tok()]
                + [pl.BlockSpec((1, LANES), lambda b, pt: (0, 0)) for _ in extra]
                + [page_spec(p) for p in range(n_pages)] * 2)
    return pl.pallas_call(
        kernel_fn,
        grid_spec=pltpu.PrefetchScalarGridSpec(
            num_scalar_prefetch=1, grid=(db,), in_specs=in_specs, out_specs=tok(),
            scratch_shapes=[pltpu.VMEM((n_pages * page, n_col), F32)]),
        out_shape=jax.ShapeDtypeStruct((db, tq, w), BF16),
        compiler_params=_cparams("arbitrary"),
        name=name,
    )(page_table.reshape(-1), *scalars, q, k_new, v_new, *extra, *([cache_k] * n_pages), *([cache_v] * n_pages))


def _post_kernel(x_ref, oa_ref, ob_ref, ga_ref, gb_ref, pe_ref, wa_ref, wb_ref, wo_ref, gmlp_ref,
                 wup_ref, wdn_ref, gple_ref, wpg_ref, wpp_ref, y_ref):
    dot = functools.partial(jnp.dot, preferred_element_type=F32)
    merged = (jax.nn.sigmoid(ga_ref[...]) * dot(oa_ref[...], wa_ref[...])
              + jax.nn.sigmoid(gb_ref[...]) * dot(ob_ref[...], wb_ref[...]))
    h = x_ref[...] + dot(merged.astype(BF16), wo_ref[...])
    u = dot(_rms(h, gmlp_ref[...]).astype(BF16), wup_ref[...])
    h = h + dot(jnp.square(jnp.maximum(u, 0.0)).astype(BF16), wdn_ref[...])
    gate = jax.nn.sigmoid(dot(_rms(h, gple_ref[...]).astype(BF16), wpg_ref[...]))
    y_ref[...] = h + gate * dot(pe_ref[...].astype(BF16), wpp_ref[...])


def _post(x, oa, ob, ga, gb, pe, weights, tm):
    m, d = x.shape
    row = lambda a: pl.BlockSpec((tm, a.shape[1]), lambda i: (i, 0))
    whole = lambda a: pl.BlockSpec(a.shape, lambda i: (0, 0), pipeline_mode=pl.Buffered(1))
    acts = (x, oa, ob, ga, gb, pe)
    return pl.pallas_call(
        _post_kernel, grid=(m // tm,),
        in_specs=[row(a) for a in acts] + [whole(a) for a in weights],
        out_specs=pl.BlockSpec((tm, d), lambda i: (i, 0)),
        out_shape=jax.ShapeDtypeStruct((m, d), F32),
        compiler_params=_cparams("parallel"),
        name="post",
    )(*acts, *weights)


def _group_projections(x2d, pos_tab, layer_w, tm):
    g_mix, w_in, gq_a, gk_a, gq_b, gk_b = layer_w
    (ca, sa), (cb, sb) = pos_tab
    xn = _input_norm(x2d, g_mix, tm)
    qa = _project(xn, w_in, 0, tm, BF16, (D_A, gq_a, ca, sa))
    ka = _project(xn, w_in, 1, tm, F32, (D_A, gk_a, ca, sa))
    va = _project(xn, w_in, 2, tm, F32)
    qb = _project(xn, w_in, 3, tm, BF16, (D_B, gq_b, cb, sb))
    kb = _project(xn, w_in, 4, tm, F32, (D_B, gk_b, cb, sb))
    vb = _project(xn, w_in, 5, tm, F32)
    ga = _project(xn, w_in, 6, tm, F32)
    gb = _project(xn, w_in, 7, tm, F32)
    return qa, ka, va, qb, kb, vb, ga, gb


def _row_tile(m, cap):
    t = min(m, cap)
    assert m % t == 0, (m, t)
    return t


def kernel(x_prompt, x_sample, p_prompt, p_sample, cache_a_k, cache_a_v, cache_b_k, cache_b_v, page_table,
           g_mix, w_in, qn_a, kn_a, lam_q1, lam_k1, lam_q2, lam_k2, g_sub_a, qn_b, kn_b,
           w_br_a, w_br_b, w_o, g_mlp, w_up, w_down, g_ple, w_ple_gate, w_ple_proj):
    b, s, d = x_prompt.shape
    db, t, _ = x_sample.shape
    depth = g_mix.shape[0]
    n_pool, page = cache_a_k.shape[1], cache_a_k.shape[2]
    n_pages = page_table.shape[1]
    past = n_pages * page
    w_a, w_b = H_A * 2 * D_A, H_B * D_B
    assert s % ATT_TILE == 0 and ATT_TILE == MOBA_BLOCK and w_a == w_b == d
    assert past % MOBA_BLOCK == 0 and MOBA_BLOCK % page == 0 and t <= MOBA_BLOCK and past // MOBA_BLOCK >= MOBA_TOPK

    pos_p = jnp.arange(s, dtype=jnp.int32)
    pos_s = past + jnp.arange(t, dtype=jnp.int32)
    tm_p = _row_tile(s, 1024)
    tm_s = _row_tile(db * t, 1024)
    assert tm_s % t == 0
    tab_p = (_rope_tables(pos_p, D_A), _rope_tables(pos_p, D_B))
    tab_s = tuple(tuple(jnp.tile(a, (tm_s // t, 1)) for a in _rope_tables(pos_s, dd)) for dd in (D_A, D_B))

    hp = x_prompt.reshape(b * s, d)
    hs = x_sample.reshape(db * t, d)
    caches = ([], [], [], [], [], [], [], [])
    tile2 = lambda g: jnp.tile(g, 2)[None, :]
    for l in range(depth):
        lam_init = 0.8 - 0.6 * math.exp(-0.3 * l)
        lam = (jnp.exp(jnp.sum(lam_q1[l] * lam_k1[l])) - jnp.exp(jnp.sum(lam_q2[l] * lam_k2[l])) + lam_init)
        lam = lam.reshape(1, 1).astype(F32)
        out_scale = 1.0 - lam_init
        layer_w = (g_mix[l][None, :], w_in[l].astype(BF16),
                   tile2(qn_a[l]) * D_A ** -0.5, tile2(kn_a[l]),
                   qn_b[l][None, :] * D_B ** -0.5, kn_b[l][None, :])
        post_w = (w_br_a[l].astype(BF16), w_br_b[l].astype(BF16), w_o[l].astype(BF16), g_mlp[l][None, :],
                  w_up[l].astype(BF16), w_down[l].astype(BF16), g_ple[l][None, :],
                  w_ple_gate[l].astype(BF16), w_ple_proj[l].astype(BF16))
        g_sub = g_sub_a[l][None, :]

        qa, ka, va, qb, kb, vb, ga, gb = _group_projections(hp, tab_p, layer_w, tm_p)
        r3 = lambda a: a.reshape(b, s, d)
        oa = _diff_attention(lam, r3(qa), r3(ka), r3(va), g_sub, out_scale).reshape(b * s, d)
        ob = _moba_attention(r3(qb), r3(kb), r3(vb)).reshape(b * s, d)
        hp = _post(hp, oa, ob, ga, gb, p_prompt[l].reshape(b * s, -1), post_w, _row_tile(b * s, 256))
        for dst, a in zip(caches[:4], (ka, va, kb, vb)):
            dst.append(a.reshape(b, s, H_A, LANES))

        qa, ka, va, qb, kb, vb, ga, gb = _group_projections(hs, tab_s, layer_w, tm_s)
        r3 = lambda a: a.reshape(db, t, d)
        c3 = lambda c: c[l].reshape(n_pool, page * H_A, LANES)
        n_col = -(-2 * H_A * t // LANES) * LANES
        oa = _sample_attention(functools.partial(_sample_diff_kernel, n_pages, out_scale), "sample_diff_attn",
                               page_table, (lam,), r3(qa), r3(ka), r3(va), (g_sub,),
                               c3(cache_a_k), c3(cache_a_v), n_col)
        n_col = -(-H_B * t // LANES) * LANES
        ob = _sample_attention(functools.partial(_sample_moba_kernel, n_pages), "sample_moba_attn",
                               page_table, (), r3(qb), r3(kb), r3(vb), (), c3(cache_b_k), c3(cache_b_v), n_col)
        hs = _post(hs, oa.reshape(db * t, d), ob.reshape(db * t, d), ga, gb, p_sample[l].reshape(db * t, -1),
                   post_w, _row_tile(db * t, 256))
        for dst, a in zip(caches[4:], (ka, va, kb, vb)):
            dst.append(a.reshape(db, t, H_A, LANES))

    return (hp.reshape(b, s, d), hs.reshape(db, t, d)) + tuple(jnp.stack(c) for c in caches)
```

```python
import functools
import math

import jax
import jax.numpy as jnp
from jax import lax
from jax.experimental import pallas as pl
from jax.experimental.pallas import tpu as pltpu

F32 = jnp.float32
BF16 = jnp.bfloat16

LANES = 128
H_A = 8
D_A = 64
H_B = 8
D_B = 128
MOBA_BLOCK = 256
MOBA_TOPK = 3
ROPE_THETA = 10000.0
EPS = 1e-6
N_SEG = 8
NEG_INF = float("-inf")
VMEM_LIMIT = 56 * 1024 * 1024
ATT_TILE = 256
DIFF_HEADS = 2
MOBA_HEADS = 4
ATT_ACC_ROWS = LANES + 16
LOG2_E = math.log2(math.e)
PROJ_ROWS = 512
POST_ROWS = 256


def _cparams(*sem):
    return pltpu.CompilerParams(dimension_semantics=sem, vmem_limit_bytes=VMEM_LIMIT)


def _rms(x, g):
    return x * lax.rsqrt(jnp.mean(x * x, axis=-1, keepdims=True) + EPS) * g


def _norm_rope(zh, g, c, s, group):
    sq = zh * zh
    lane = lax.broadcasted_iota(jnp.int32, zh.shape, 1)
    if group == LANES:
        ss = jnp.sum(sq, axis=-1, keepdims=True)
    else:
        lo = lane < group
        s_lo = jnp.sum(jnp.where(lo, sq, 0.0), axis=-1, keepdims=True)
        s_all = jnp.sum(sq, axis=-1, keepdims=True)
        ss = jnp.where(lo, s_lo, s_all - s_lo)
    y = zh * lax.rsqrt(ss * (1.0 / group) + EPS) * g
    half = group // 2
    if group == LANES:
        sw = pltpu.roll(y, half, axis=1)
    else:
        first = (lane % group) < half
        sw = jnp.where(first, pltpu.roll(y, LANES - half, axis=1), pltpu.roll(y, half, axis=1))
    return y * c + sw * s


def _proj_kernel(x_ref, gmix_ref, w_ref, gqa_ref, gka_ref, gqb_ref, gkb_ref, ca_ref, sa_ref, cb_ref, sb_ref,
                 qa_ref, ka_ref, va_ref, qb_ref, kb_ref, vb_ref, ka16_ref, va16_ref, kb16_ref, vb16_ref):
    xn = _rms(x_ref[...], gmix_ref[...]).astype(BF16)
    width = qa_ref.shape[1]

    def segment(i):
        return jnp.dot(xn, w_ref[:, i * width:(i + 1) * width], preferred_element_type=F32)

    def store_rope(o_refs, z, group, g_ref, c_ref, s_ref):
        g, c, s = g_ref[...], c_ref[...], s_ref[...]
        for h in range(width // LANES):
            cols = slice(h * LANES, (h + 1) * LANES)
            y = _norm_rope(z[:, cols], g, c, s, group)
            for o_ref in o_refs:
                o_ref[:, cols] = y.astype(o_ref.dtype)

    def store_plain(o_refs, z):
        for o_ref in o_refs:
            o_ref[...] = z.astype(o_ref.dtype)

    store_rope((qa_ref,), segment(0), D_A, gqa_ref, ca_ref, sa_ref)
    store_rope((ka_ref, ka16_ref), segment(1), D_A, gka_ref, ca_ref, sa_ref)
    store_plain((va_ref, va16_ref), segment(2))
    store_rope((qb_ref,), segment(3), D_B, gqb_ref, cb_ref, sb_ref)
    store_rope((kb_ref, kb16_ref), segment(4), D_B, gkb_ref, cb_ref, sb_ref)
    store_plain((vb_ref, vb16_ref), segment(5))


def _project(x, g_mix, w_qkv, gains, tables, tm):
    m, d = x.shape
    width = w_qkv.shape[1] // 6
    n_tab = tables[0].shape[0] // tm
    row = lambda w: pl.BlockSpec((tm, w), lambda i: (i, 0))
    const = lambda a: pl.BlockSpec(a.shape, lambda i: (0, 0), pipeline_mode=pl.Buffered(1))
    tab = pl.BlockSpec((tm, LANES), lambda i: (i % n_tab, 0))
    out_dtypes = (BF16, F32, F32, BF16, F32, F32, BF16, BF16, BF16, BF16)
    return pl.pallas_call(
        _proj_kernel, grid=(m // tm,),
        in_specs=[row(d), const(g_mix), const(w_qkv)] + [const(g) for g in gains] + [tab] * 4,
        out_specs=[row(width)] * len(out_dtypes),
        out_shape=[jax.ShapeDtypeStruct((m, width), dt) for dt in out_dtypes],
        compiler_params=_cparams("parallel"),
        name="proj",
    )(x, g_mix, w_qkv, *gains, *tables)


def _rope_tables(pos, d):
    half = d // 2
    inv = ROPE_THETA ** (-jnp.arange(half, dtype=F32) * 2.0 / d)
    ang = pos.astype(F32)[:, None] * inv[None, :]
    cos, sin = jnp.cos(ang), jnp.sin(ang)
    reps = LANES // d
    c = jnp.tile(jnp.concatenate([cos, cos], axis=-1), (1, reps))
    s = jnp.tile(jnp.concatenate([-sin, sin], axis=-1), (1, reps))
    return c, s


def _stage_transposed(src_ref, g, dst_scr, n_tiles, row_mask=None):
    cols = slice(g * LANES, (g + 1) * LANES)
    extra = dst_scr.shape[2] - LANES
    for i in range(n_tiles):
        t = src_ref[i * ATT_TILE:(i + 1) * ATT_TILE, cols].astype(F32).T
        if row_mask is not None:
            t = jnp.where(row_mask, t, 0.0)
        dst_scr[g, i, 0:LANES, :] = t.astype(BF16)
        if extra:
            dst_scr[g, i, LANES:LANES + extra, :] = jnp.ones((extra, ATT_TILE), BF16)


def _causal_blocks(qi, n_chains, acc_scr, park_a, park_b, scores, values_t):
    t = ATT_TILE
    chains = range(n_chains)

    def park(j, dst_scr, diagonal):
        for c, s in enumerate(scores(j, diagonal)):
            dst_scr[c] = s

    def consume(j, src_scr, maxes):
        out, pend = [], []
        for c in chains:
            s = src_scr[c]
            m_new = jnp.maximum(maxes[c], jnp.max(s, axis=0, keepdims=True))
            p = jnp.exp2(s - m_new).astype(BF16)
            pend.append((jnp.exp2(maxes[c] - m_new), jnp.dot(values_t(c, j), p, preferred_element_type=F32)))
            out.append(m_new)
        return tuple(out), pend

    def accumulate(*pending):
        for c in chains:
            acc = acc_scr[c]
            for pend in pending:
                a, pv = pend[c]
                acc = acc * a + pv
            acc_scr[c] = acc

    acc_scr[...] = jnp.zeros_like(acc_scr)
    init = (jnp.full((1, t), NEG_INF, F32),) * n_chains
    park(qi, park_a, True)

    def pair(pi, stats):
        j = 2 * pi
        park(j, park_b, False)
        stats, pend_a = consume(jnp.where(pi == 0, qi, j - 1), park_a, stats)
        park(j + 1, park_a, False)
        stats, pend_b = consume(j, park_b, stats)
        accumulate(pend_a, pend_b)
        return stats

    n_pairs = qi // 2
    stats = lax.fori_loop(0, n_pairs, pair, init)
    in_a = jnp.where(n_pairs == 0, qi, 2 * n_pairs - 1)

    def odd_tail(stats):
        park(qi - 1, park_b, False)
        stats, pend_a = consume(in_a, park_a, stats)
        stats, pend_b = consume(qi - 1, park_b, stats)
        accumulate(pend_a, pend_b)
        return stats

    def even_tail(stats):
        stats, pend_a = consume(in_a, park_a, stats)
        accumulate(pend_a)
        return stats

    lax.cond(qi % 2 == 1, odd_tail, even_tail, stats)


def _normalised(acc_scr, c):
    return acc_scr[c, 0:LANES, :] / acc_scr[c, LANES:LANES + 1, :]


def _key_block(k_ref, g, j):
    return k_ref[pl.ds(pl.multiple_of(j * ATT_TILE, ATT_TILE), ATT_TILE), g * LANES:(g + 1) * LANES]


def _diff_attn_kernel(lam_ref, q_ref, k_ref, v_ref, gsub_ref, o_ref,
                      q1t_scr, q2t_scr, vt_scr, acc_scr, park_a, park_b, *, out_scale):
    t = ATT_TILE
    n_tiles = q_ref.shape[0] // t
    heads = range(DIFF_HEADS)
    row = lax.broadcasted_iota(jnp.int32, (LANES, t), 0)
    for g in heads:
        _stage_transposed(q_ref, g, q1t_scr, n_tiles, row < D_A)
        _stage_transposed(q_ref, g, q2t_scr, n_tiles, row >= D_A)
        _stage_transposed(v_ref, g, vt_scr, n_tiles)
    lam = lam_ref[0, 0]
    causal = (lax.broadcasted_iota(jnp.int32, (t, t), 0) <= lax.broadcasted_iota(jnp.int32, (t, t), 1))

    def q_tile(qi, carry):

        def scores(j, diagonal):
            out = []
            for g in heads:
                kj = _key_block(k_ref, g, j)
                for qt_scr in (q1t_scr, q2t_scr):
                    s = jnp.dot(kj, qt_scr[g, qi], preferred_element_type=F32)
                    out.append(jnp.where(causal, s, NEG_INF) if diagonal else s)
            return out

        _causal_blocks(qi, 2 * DIFF_HEADS, acc_scr, park_a, park_b, scores, lambda c, j: vt_scr[c // 2, j])
        for g in heads:
            ot = _normalised(acc_scr, 2 * g) - lam * _normalised(acc_scr, 2 * g + 1)
            o = _rms(ot.T, gsub_ref[...]) * out_scale
            o_ref[pl.ds(pl.multiple_of(qi * t, t), t), g * LANES:(g + 1) * LANES] = o.astype(o_ref.dtype)
        return carry

    lax.fori_loop(0, n_tiles, q_tile, 0)


def _head_group_spec(s, heads, single_buffered=False):
    mode = dict(pipeline_mode=pl.Buffered(1)) if single_buffered else {}
    return pl.BlockSpec((None, s, heads * LANES), lambda bi, h: (bi, 0, h), **mode)


def _diff_attention(lam, q, k, v, g_sub, out_scale):
    b, s, w = q.shape
    n_tiles = s // ATT_TILE
    hg = DIFF_HEADS
    tiles_t = pltpu.VMEM((hg, n_tiles, LANES, ATT_TILE), BF16)
    values_t = pltpu.VMEM((hg, n_tiles, ATT_ACC_ROWS, ATT_TILE), BF16)
    parking = pltpu.VMEM((2 * hg, ATT_TILE, ATT_TILE), F32)
    return pl.pallas_call(
        functools.partial(_diff_attn_kernel, out_scale=out_scale),
        grid=(b, w // (hg * LANES)),
        in_specs=[pl.BlockSpec(memory_space=pltpu.SMEM)] + [_head_group_spec(s, hg, True)] * 3
        + [pl.BlockSpec((1, LANES), lambda bi, h: (0, 0))],
        out_specs=_head_group_spec(s, hg),
        out_shape=jax.ShapeDtypeStruct((b, s, w), BF16),
        scratch_shapes=[tiles_t, tiles_t, values_t, pltpu.VMEM((2 * hg, ATT_ACC_ROWS, ATT_TILE), F32),
                        parking, parking],
        compiler_params=_cparams("parallel", "parallel"),
        name="diff_attn",
    )(lam, q, k, v, g_sub)


def _split_hi_lo(x):
    hi = x.astype(BF16)
    return hi, (x - hi.astype(F32)).astype(BF16)


def _topk_bias(g, n_valid):
    nb = g.shape[0]
    idx = lax.broadcasted_iota(jnp.int32, g.shape, 0)
    rank = jnp.zeros(g.shape, jnp.int32)
    for m in range(nb):
        gm = g[m:m + 1, :]
        beats = jnp.where(gm > g, 1, jnp.where((gm == g) & (idx > m), 1, 0))
        rank = rank + jnp.where(n_valid > m, beats, 0)
    sel = (idx < n_valid) & (rank < MOBA_TOPK)
    return jnp.where(sel, 0.0, NEG_INF)


def _moba_kernel(q_ref, k_ref, v_ref, o_ref, qt_scr, vt_scr, kmh_scr, kml_scr, bias_scr, acc_scr,
                 park_a, park_b):
    t = ATT_TILE
    n_tiles = q_ref.shape[0] // t
    heads = range(MOBA_HEADS)
    for g in heads:
        _stage_transposed(q_ref, g, qt_scr, n_tiles)
        _stage_transposed(v_ref, g, vt_scr, n_tiles)
        k = k_ref[:, g * LANES:(g + 1) * LANES].astype(F32)
        kmh_scr[g], kml_scr[g] = _split_hi_lo(jnp.mean(k.reshape(n_tiles, t, LANES), axis=1))
    causal = (lax.broadcasted_iota(jnp.int32, (t, t), 0) <= lax.broadcasted_iota(jnp.int32, (t, t), 1))

    def q_tile(qi, carry):
        for g in heads:
            qt = qt_scr[g, qi]
            gate = (jnp.dot(kmh_scr[g], qt, preferred_element_type=F32)
                    + jnp.dot(kml_scr[g], qt, preferred_element_type=F32))
            bias = _topk_bias(gate, qi)
            for n in range(n_tiles):
                bias_scr[g, n] = bias[n:n + 1, :]

        def scores(j, diagonal):
            out = []
            for g in heads:
                s = jnp.dot(_key_block(k_ref, g, j), qt_scr[g, qi], preferred_element_type=F32)
                out.append(jnp.where(causal, s, NEG_INF) if diagonal else s + bias_scr[g, j])
            return out

        _causal_blocks(qi, MOBA_HEADS, acc_scr, park_a, park_b, scores, lambda c, j: vt_scr[c, j])
        for g in heads:
            o = _normalised(acc_scr, g).T
            o_ref[pl.ds(pl.multiple_of(qi * t, t), t), g * LANES:(g + 1) * LANES] = o.astype(o_ref.dtype)
        return carry

    lax.fori_loop(0, n_tiles, q_tile, 0)


def _moba_attention(q, k, v):
    b, s, w = q.shape
    n_tiles = s // ATT_TILE
    hg = MOBA_HEADS
    tiles_t = pltpu.VMEM((hg, n_tiles, LANES, ATT_TILE), BF16)
    values_t = pltpu.VMEM((hg, n_tiles, ATT_ACC_ROWS, ATT_TILE), BF16)
    means = pltpu.VMEM((hg, n_tiles, LANES), BF16)
    parking = pltpu.VMEM((hg, ATT_TILE, ATT_TILE), F32)
    return pl.pallas_call(
        _moba_kernel,
        grid=(b, w // (hg * LANES)),
        in_specs=[_head_group_spec(s, hg, True)] * 3,
        out_specs=_head_group_spec(s, hg),
        out_shape=jax.ShapeDtypeStruct((b, s, w), BF16),
        scratch_shapes=[tiles_t, values_t, means, means, pltpu.VMEM((hg, n_tiles, 1, ATT_TILE), F32),
                        pltpu.VMEM((hg, ATT_ACC_ROWS, ATT_TILE), F32), parking, parking],
        compiler_params=_cparams("parallel", "parallel"),
        name="moba_attn",
    )(q, k, v)


def _block_diag_queries(q, n_rep):
    t, w = q.shape
    used = n_rep * (w // LANES) * t
    rows = -(-used // LANES) * LANES
    tiled = jnp.concatenate([q] * (rows // t), axis=0)
    r = lax.broadcasted_iota(jnp.int32, (rows, w), 0)
    c = lax.broadcasted_iota(jnp.int32, (rows, w), 1)
    head = (r // t) % (w // LANES)
    part = r // (t * (w // LANES))
    keep = (c // LANES == head) & ((c % LANES) // (LANES // n_rep) == part) & (r < used)
    return jnp.where(keep, tiled, 0.0).astype(BF16)


def _dot_nt(a, b):
    return lax.dot_general(a, b, (((1,), (1,)), ((), ())), preferred_element_type=F32)


def _dot_tn(a, b):
    return lax.dot_general(a, b, (((0,), (0,)), ((), ())), preferred_element_type=F32)


def _head_diagonal(x, t):
    return jnp.concatenate(
        [x[h * t:(h + 1) * t, h * LANES:(h + 1) * LANES] for h in range(x.shape[1] // LANES)], axis=1)


def _load_page(ref):
    n_head = H_A
    slots = ref.shape[0] // n_head
    return jnp.concatenate([ref[pl.ds(h, slots, stride=n_head), :] for h in range(n_head)], axis=1)


def _new_token_scores(kn_ref, qbd, tq):
    s_new = _dot_nt(kn_ref[...].astype(BF16), qbd)
    key_t = lax.broadcasted_iota(jnp.int32, s_new.shape, 0)
    qry_t = lax.broadcasted_iota(jnp.int32, s_new.shape, 1) % tq
    return jnp.where(key_t <= qry_t, s_new, NEG_INF)


def _softmax_times_values(s_scr, s_new, m, vn_ref, v_pages):
    page = v_pages[0].shape[0] // H_A
    p_new = jnp.exp2(s_new - m)
    l = jnp.sum(p_new, axis=0, keepdims=True)
    for p in range(len(v_pages)):
        rows = slice(p * page, (p + 1) * page)
        pp = jnp.exp2(s_scr[rows, :] - m)
        s_scr[rows, :] = pp
        l = l + jnp.sum(pp, axis=0, keepdims=True)
    inv = 1.0 / l
    acc = _dot_tn((p_new * inv).astype(BF16), vn_ref[...].astype(BF16))
    for p in range(len(v_pages)):
        pp = s_scr[p * page:(p + 1) * page, :] * inv
        acc = acc + _dot_tn(pp.astype(BF16), _load_page(v_pages[p]).astype(BF16))
    return acc


def _sample_diff_kernel(n_pages, out_scale, pt_ref, lam_ref, q_ref, kn_ref, vn_ref, gsub_ref, *rest):
    k_pages = rest[:n_pages]
    v_pages = rest[n_pages:2 * n_pages]
    o_ref, s_scr = rest[2 * n_pages], rest[2 * n_pages + 1]
    tq = q_ref.shape[0]
    page = k_pages[0].shape[0] // H_A
    qbd = _block_diag_queries(q_ref[...].astype(F32), 2)
    s_new = _new_token_scores(kn_ref, qbd, tq)
    m = jnp.max(s_new, axis=0, keepdims=True)
    for p in range(n_pages):
        sp = _dot_nt(_load_page(k_pages[p]).astype(BF16), qbd)
        s_scr[p * page:(p + 1) * page, :] = sp
        m = jnp.maximum(m, jnp.max(sp, axis=0, keepdims=True))
    acc = _softmax_times_values(s_scr, s_new, m, vn_ref, v_pages)
    half = (q_ref.shape[1] // LANES) * tq
    o = _head_diagonal(acc[:half] - lam_ref[0, 0] * acc[half:2 * half], tq)
    g = gsub_ref[...]
    for h in range(o.shape[1] // LANES):
        cols = slice(h * LANES, (h + 1) * LANES)
        o_ref[:, cols] = (_rms(o[:, cols], g) * out_scale).astype(o_ref.dtype)


def _sample_moba_kernel(n_pages, pt_ref, q_ref, kn_ref, vn_ref, *rest):
    k_pages = rest[:n_pages]
    v_pages = rest[n_pages:2 * n_pages]
    o_ref, s_scr = rest[2 * n_pages], rest[2 * n_pages + 1]
    tq = q_ref.shape[0]
    page = k_pages[0].shape[0] // H_B
    per_blk = MOBA_BLOCK // page
    n_blk = n_pages // per_blk
    qbd = _block_diag_queries(q_ref[...].astype(F32), 1)
    means = []
    for n in range(n_blk):
        tot = jnp.zeros((1, q_ref.shape[1]), F32)
        for p in range(n * per_blk, (n + 1) * per_blk):
            kp = _load_page(k_pages[p])
            tot = tot + jnp.sum(kp, axis=0, keepdims=True)
            s_scr[p * page:(p + 1) * page, :] = _dot_nt(kp.astype(BF16), qbd)
        means.append(tot * (1.0 / MOBA_BLOCK))
    km_hi, km_lo = _split_hi_lo(jnp.concatenate(means, axis=0))
    bias = _topk_bias(_dot_nt(km_hi, qbd) + _dot_nt(km_lo, qbd), n_blk)
    s_new = _new_token_scores(kn_ref, qbd, tq)
    m = jnp.max(s_new, axis=0, keepdims=True)
    for p in range(n_pages):
        rows = slice(p * page, (p + 1) * page)
        sp = s_scr[rows, :] + bias[p // per_blk:p // per_blk + 1, :]
        s_scr[rows, :] = sp
        m = jnp.maximum(m, jnp.max(sp, axis=0, keepdims=True))
    acc = _softmax_times_values(s_scr, s_new, m, vn_ref, v_pages)
    o_ref[...] = _head_diagonal(acc, tq).astype(o_ref.dtype)


def _sample_attention(kernel_fn, name, page_table, scalars, q, k_new, v_new, extra, cache_k, cache_v, n_col):
    db, tq, w = q.shape
    n_pages = page_table.shape[1]
    page_rows = cache_k.shape[1]
    page = page_rows // (w // LANES)
    tok = lambda: pl.BlockSpec((None, tq, w), lambda b, pt: (b, 0, 0))
    page_spec = lambda p: pl.BlockSpec((None, page_rows, LANES), lambda b, pt: (pt[b * n_pages + p], 0, 0))
    in_specs = ([pl.BlockSpec(memory_space=pltpu.SMEM) for _ in scalars] + [tok(), tok(), tok()]
                + [pl.BlockSpec((1, LANES), lambda b, pt: (0, 0)) for _ in extra]
                + [page_spec(p) for p in range(n_pages)] * 2)
    return pl.pallas_call(
        kernel_fn,
        grid_spec=pltpu.PrefetchScalarGridSpec(
            num_scalar_prefetch=1, grid=(db,), in_specs=in_specs, out_specs=tok(),
            scratch_shapes=[pltpu.VMEM((n_pages * page, n_col), F32)]),
        out_shape=jax.ShapeDtypeStruct((db, tq, w), BF16),
        compiler_params=_cparams("arbitrary"),
        name=name,
    )(page_table.reshape(-1), *scalars, q, k_new, v_new, *extra, *([cache_k] * n_pages), *([cache_v] * n_pages))


def _post_kernel(x_ref, oa_ref, ob_ref, pe_ref, gmix_ref, wg_ref, wa_ref, wb_ref, wo_ref, gmlp_ref,
                 wup_ref, wdn_ref, gple_ref, wpg_ref, wpp_ref, y_ref):
    dot = functools.partial(jnp.dot, preferred_element_type=F32)
    x = x_ref[...]
    d = x.shape[1]
    xn = _rms(x, gmix_ref[...]).astype(BF16)
    merged = (jax.nn.sigmoid(dot(xn, wg_ref[:, :d])) * dot(oa_ref[...], wa_ref[...])
              + jax.nn.sigmoid(dot(xn, wg_ref[:, d:])) * dot(ob_ref[...], wb_ref[...]))
    h = x + dot(merged.astype(BF16), wo_ref[...])
    u = dot(_rms(h, gmlp_ref[...]).astype(BF16), wup_ref[...])
    h = h + dot(jnp.square(jnp.maximum(u, 0.0)).astype(BF16), wdn_ref[...])
    gate = jax.nn.sigmoid(dot(_rms(h, gple_ref[...]).astype(BF16), wpg_ref[...]))
    y_ref[...] = h + gate * dot(pe_ref[...].astype(BF16), wpp_ref[...])


def _post(x, oa, ob, pe, weights, tm):
    m, d = x.shape
    row = lambda a: pl.BlockSpec((tm, a.shape[1]), lambda i: (i, 0))
    whole = lambda a: pl.BlockSpec(a.shape, lambda i: (0, 0), pipeline_mode=pl.Buffered(1))
    acts = (x, oa, ob, pe)
    return pl.pallas_call(
        _post_kernel, grid=(m // tm,),
        in_specs=[row(a) for a in acts] + [whole(a) for a in weights],
        out_specs=pl.BlockSpec((tm, d), lambda i: (i, 0)),
        out_shape=jax.ShapeDtypeStruct((m, d), F32),
        compiler_params=_cparams("parallel"),
        name="post",
    )(*acts, *weights)


def _row_tile(m, cap):
    t = min(m, cap)
    assert m % t == 0, (m, t)
    return t


def kernel(x_prompt, x_sample, p_prompt, p_sample, cache_a_k, cache_a_v, cache_b_k, cache_b_v, page_table,
           g_mix, w_in, qn_a, kn_a, lam_q1, lam_k1, lam_q2, lam_k2, g_sub_a, qn_b, kn_b,
           w_br_a, w_br_b, w_o, g_mlp, w_up, w_down, g_ple, w_ple_gate, w_ple_proj):
    b, s, d = x_prompt.shape
    db, t, _ = x_sample.shape
    depth = g_mix.shape[0]
    n_pool, page = cache_a_k.shape[1], cache_a_k.shape[2]
    n_pages = page_table.shape[1]
    past = n_pages * page
    w_a, w_b = H_A * 2 * D_A, H_B * D_B
    assert s % ATT_TILE == 0 and ATT_TILE == MOBA_BLOCK and w_a == w_b == d and H_A == H_B
    assert past % MOBA_BLOCK == 0 and MOBA_BLOCK % page == 0 and t <= MOBA_BLOCK and past // MOBA_BLOCK >= MOBA_TOPK
    assert w_in.shape[2] == N_SEG * d

    pos_p = jnp.arange(s, dtype=jnp.int32)
    pos_s = past + jnp.arange(t, dtype=jnp.int32)
    tm_p = _row_tile(s, PROJ_ROWS)
    tm_s = _row_tile(db * t, PROJ_ROWS)
    assert tm_s % t == 0
    tab_p = _rope_tables(pos_p, D_A) + _rope_tables(pos_p, D_B)
    tab_s = tuple(jnp.tile(a, (tm_s // t, 1)) for dd in (D_A, D_B) for a in _rope_tables(pos_s, dd))

    hp = x_prompt.reshape(b * s, d)
    hs = x_sample.reshape(db * t, d)
    caches = ([], [], [], [], [], [], [], [])
    tile2 = lambda g: jnp.tile(g, 2)[None, :]
    for l in range(depth):
        lam_init = 0.8 - 0.6 * math.exp(-0.3 * l)
        lam = (jnp.exp(jnp.sum(lam_q1[l] * lam_k1[l])) - jnp.exp(jnp.sum(lam_q2[l] * lam_k2[l])) + lam_init)
        lam = lam.reshape(1, 1).astype(F32)
        out_scale = 1.0 - lam_init
        g_mix_l = g_mix[l][None, :]
        w_qkv = w_in[l][:, :6 * d].astype(BF16)
        gains = (tile2(qn_a[l]) * (D_A ** -0.5 * LOG2_E), tile2(kn_a[l]),
                 qn_b[l][None, :] * (D_B ** -0.5 * LOG2_E), kn_b[l][None, :])
        post_w = (g_mix_l, w_in[l][:, 6 * d:].astype(BF16),
                  w_br_a[l].astype(BF16), w_br_b[l].astype(BF16), w_o[l].astype(BF16), g_mlp[l][None, :],
                  w_up[l].astype(BF16), w_down[l].astype(BF16), g_ple[l][None, :],
                  w_ple_gate[l].astype(BF16), w_ple_proj[l].astype(BF16))
        g_sub = g_sub_a[l][None, :]

        qa, ka, va, qb, kb, vb, ka16, va16, kb16, vb16 = _project(hp, g_mix_l, w_qkv, gains, tab_p, tm_p)
        r3 = lambda a: a.reshape(b, s, d)
        oa = _diff_attention(lam, r3(qa), r3(ka16), r3(va16), g_sub, out_scale).reshape(b * s, d)
        ob = _moba_attention(r3(qb), r3(kb16), r3(vb16)).reshape(b * s, d)
        hp = _post(hp, oa, ob, p_prompt[l].reshape(b * s, -1), post_w, _row_tile(b * s, POST_ROWS))
        for dst, a in zip(caches[:4], (ka, va, kb, vb)):
            dst.append(a.reshape(b, s, H_A, LANES))

        qa, ka, va, qb, kb, vb = _project(hs, g_mix_l, w_qkv, gains, tab_s, tm_s)[:6]
        r3 = lambda a: a.reshape(db, t, d)
        c3 = lambda c: c[l].reshape(n_pool, page * H_A, LANES)
        n_col = -(-2 * H_A * t // LANES) * LANES
        oa = _sample_attention(functools.partial(_sample_diff_kernel, n_pages, out_scale), "sample_diff_attn",
                               page_table, (lam,), r3(qa), r3(ka), r3(va), (g_sub,),
                               c3(cache_a_k), c3(cache_a_v), n_col)
        n_col = -(-H_B * t // LANES) * LANES
        ob = _sample_attention(functools.partial(_sample_moba_kernel, n_pages), "sample_moba_attn",
                               page_table, (), r3(qb), r3(kb), r3(vb), (), c3(cache_b_k), c3(cache_b_v), n_col)
        hs = _post(hs, oa.reshape(db * t, d), ob.reshape(db * t, d), p_sample[l].reshape(db * t, -1),
                   post_w, _row_tile(db * t, POST_ROWS))
        for dst, a in zip(caches[4:], (ka, va, kb, vb)):
            dst.append(a.reshape(db, t, H_A, LANES))

    return (hp.reshape(b, s, d), hs.reshape(db, t, d)) + tuple(jnp.stack(c) for c in caches)
```

```python
import functools
import math

import jax
import jax.numpy as jnp
from jax import lax
from jax.experimental import pallas as pl
from jax.experimental.pallas import tpu as pltpu

F32 = jnp.float32
BF16 = jnp.bfloat16

LANES = 128
H_A = 8
D_A = 64
H_B = 8
D_B = 128
MOBA_BLOCK = 256
MOBA_TOPK = 3
ROPE_THETA = 10000.0
EPS = 1e-6
N_SEG = 8
N_QKV_SEG = 6
NEG_INF = float("-inf")
VMEM_LIMIT = 56 * 1024 * 1024
ATT_TILE = 256
DIFF_HEADS = 4
MOBA_HEADS = 4
ATT_ACC_ROWS = LANES + 16
LOG2_E = math.log2(math.e)
PROJ_ROWS = 512
POST_ROWS = 256


def _cparams(*sem):
    return pltpu.CompilerParams(dimension_semantics=sem, vmem_limit_bytes=VMEM_LIMIT)


def _rms(x, g):
    return x * lax.rsqrt(jnp.mean(x * x, axis=-1, keepdims=True) + EPS) * g


def _norm_rope(zh, g, c, s, group):
    sq = zh * zh
    lane = lax.broadcasted_iota(jnp.int32, zh.shape, 1)
    if group == LANES:
        ss = jnp.sum(sq, axis=-1, keepdims=True)
    else:
        lo = lane < group
        s_lo = jnp.sum(jnp.where(lo, sq, 0.0), axis=-1, keepdims=True)
        s_all = jnp.sum(sq, axis=-1, keepdims=True)
        ss = jnp.where(lo, s_lo, s_all - s_lo)
    y = zh * lax.rsqrt(ss * (1.0 / group) + EPS) * g
    half = group // 2
    if group == LANES:
        sw = pltpu.roll(y, half, axis=1)
    else:
        first = (lane % group) < half
        sw = jnp.where(first, pltpu.roll(y, LANES - half, axis=1), pltpu.roll(y, half, axis=1))
    return y * c + sw * s


def _proj_kernel(x_ref, gmix_ref, w_ref, gqa_ref, gka_ref, gqb_ref, gkb_ref, ca_ref, sa_ref, cb_ref, sb_ref,
                 qa_ref, ka_ref, va_ref, qb_ref, kb_ref, vb_ref, ka16_ref, va16_ref, kb16_ref, vb16_ref):
    xn = _rms(x_ref[...], gmix_ref[...]).astype(BF16)
    width = qa_ref.shape[1]

    def segment(i):
        return jnp.dot(xn, w_ref[:, i * width:(i + 1) * width], preferred_element_type=F32)

    def store_rope(o_refs, z, group, g_ref, c_ref, s_ref):
        g, c, s = g_ref[...], c_ref[...], s_ref[...]
        for h in range(width // LANES):
            cols = slice(h * LANES, (h + 1) * LANES)
            y = _norm_rope(z[:, cols], g, c, s, group)
            for o_ref in o_refs:
                o_ref[:, cols] = y.astype(o_ref.dtype)

    def store_plain(o_refs, z):
        for o_ref in o_refs:
            o_ref[...] = z.astype(o_ref.dtype)

    store_rope((qa_ref,), segment(0), D_A, gqa_ref, ca_ref, sa_ref)
    store_rope((ka_ref, ka16_ref), segment(1), D_A, gka_ref, ca_ref, sa_ref)
    store_plain((va_ref, va16_ref), segment(2))
    store_rope((qb_ref,), segment(3), D_B, gqb_ref, cb_ref, sb_ref)
    store_rope((kb_ref, kb16_ref), segment(4), D_B, gkb_ref, cb_ref, sb_ref)
    store_plain((vb_ref, vb16_ref), segment(5))


def _project(x, g_mix, w_in16, gains, tables, tm):
    m, d = x.shape
    width = w_in16.shape[1] // N_SEG
    n_tab = tables[0].shape[0] // tm
    row = lambda w: pl.BlockSpec((tm, w), lambda i: (i, 0))
    const = lambda a: pl.BlockSpec(a.shape, lambda i: (0, 0), pipeline_mode=pl.Buffered(1))
    w_spec = pl.BlockSpec((d, N_QKV_SEG * width), lambda i: (0, 0), pipeline_mode=pl.Buffered(1))
    tab = pl.BlockSpec((tm, LANES), lambda i: (i % n_tab, 0))
    out_dtypes = (BF16, F32, F32, BF16, F32, F32, BF16, BF16, BF16, BF16)
    return pl.pallas_call(
        _proj_kernel, grid=(m // tm,),
        in_specs=[row(d), const(g_mix), w_spec] + [const(g) for g in gains] + [tab] * 4,
        out_specs=[row(width)] * len(out_dtypes),
        out_shape=[jax.ShapeDtypeStruct((m, width), dt) for dt in out_dtypes],
        compiler_params=_cparams("parallel"),
        name="proj",
    )(x, g_mix, w_in16, *gains, *tables)


def _rope_tables(pos, d):
    half = d // 2
    inv = ROPE_THETA ** (-jnp.arange(half, dtype=F32) * 2.0 / d)
    ang = pos.astype(F32)[:, None] * inv[None, :]
    cos, sin = jnp.cos(ang), jnp.sin(ang)
    reps = LANES // d
    c = jnp.tile(jnp.concatenate([cos, cos], axis=-1), (1, reps))
    s = jnp.tile(jnp.concatenate([-sin, sin], axis=-1), (1, reps))
    return c, s


def _stage_transposed(src_ref, g, dst_scr, n_tiles, row_mask=None):
    cols = slice(g * LANES, (g + 1) * LANES)
    extra = dst_scr.shape[2] - LANES
    for i in range(n_tiles):
        t = src_ref[i * ATT_TILE:(i + 1) * ATT_TILE, cols].astype(F32).T
        if row_mask is not None:
            t = jnp.where(row_mask, t, 0.0)
        dst_scr[g, i, 0:LANES, :] = t.astype(BF16)
        if extra:
            dst_scr[g, i, LANES:LANES + extra, :] = jnp.ones((extra, ATT_TILE), BF16)


def _causal_blocks(qi, n_chains, acc_scr, park_a, park_b, scores, values_t):
    t = ATT_TILE
    chains = range(n_chains)

    def park(j, dst_scr, diagonal):
        for c, s in enumerate(scores(j, diagonal)):
            dst_scr[c] = s

    def consume(j, src_scr, maxes):
        out, pend = [], []
        for c in chains:
            s = src_scr[c]
            m_new = jnp.maximum(maxes[c], jnp.max(s, axis=0, keepdims=True))
            p = jnp.exp2(s - m_new).astype(BF16)
            pend.append((jnp.exp2(maxes[c] - m_new), jnp.dot(values_t(c, j), p, preferred_element_type=F32)))
            out.append(m_new)
        return tuple(out), pend

    def accumulate(*pending):
        for c in chains:
            acc = acc_scr[c]
            for pend in pending:
                a, pv = pend[c]
                acc = acc * a + pv
            acc_scr[c] = acc

    acc_scr[...] = jnp.zeros_like(acc_scr)
    init = (jnp.full((1, t), NEG_INF, F32),) * n_chains
    park(qi, park_a, True)

    def pair(pi, stats):
        j = 2 * pi
        park(j, park_b, False)
        stats, pend_a = consume(jnp.where(pi == 0, qi, j - 1), park_a, stats)
        park(j + 1, park_a, False)
        stats, pend_b = consume(j, park_b, stats)
        accumulate(pend_a, pend_b)
        return stats

    n_pairs = qi // 2
    stats = lax.fori_loop(0, n_pairs, pair, init)
    in_a = jnp.where(n_pairs == 0, qi, 2 * n_pairs - 1)

    def odd_tail(stats):
        park(qi - 1, park_b, False)
        stats, pend_a = consume(in_a, park_a, stats)
        stats, pend_b = consume(qi - 1, park_b, stats)
        accumulate(pend_a, pend_b)
        return stats

    def even_tail(stats):
        stats, pend_a = consume(in_a, park_a, stats)
        accumulate(pend_a)
        return stats

    lax.cond(qi % 2 == 1, odd_tail, even_tail, stats)


def _normalised(acc_scr, c):
    return acc_scr[c, 0:LANES, :] / acc_scr[c, LANES:LANES + 1, :]


def _key_block(k_ref, g, j):
    return k_ref[pl.ds(pl.multiple_of(j * ATT_TILE, ATT_TILE), ATT_TILE), g * LANES:(g + 1) * LANES]


def _diff_attn_kernel(lam_ref, q_ref, k_ref, v_ref, gsub_ref, o_ref,
                      q1t_scr, q2t_scr, vt_scr, acc_scr, park_a, park_b, *, out_scale):
    t = ATT_TILE
    n_tiles = q_ref.shape[0] // t
    heads = range(DIFF_HEADS)
    row = lax.broadcasted_iota(jnp.int32, (LANES, t), 0)
    for g in heads:
        _stage_transposed(q_ref, g, q1t_scr, n_tiles, row < D_A)
        _stage_transposed(q_ref, g, q2t_scr, n_tiles, row >= D_A)
        _stage_transposed(v_ref, g, vt_scr, n_tiles)
    lam = lam_ref[0, 0]
    causal = (lax.broadcasted_iota(jnp.int32, (t, t), 0) <= lax.broadcasted_iota(jnp.int32, (t, t), 1))

    def q_tile(qi, carry):

        def scores(j, diagonal):
            out = []
            for g in heads:
                kj = _key_block(k_ref, g, j)
                for qt_scr in (q1t_scr, q2t_scr):
                    s = jnp.dot(kj, qt_scr[g, qi], preferred_element_type=F32)
                    out.append(jnp.where(causal, s, NEG_INF) if diagonal else s)
            return out

        _causal_blocks(qi, 2 * DIFF_HEADS, acc_scr, park_a, park_b, scores, lambda c, j: vt_scr[c // 2, j])
        for g in heads:
            ot = _normalised(acc_scr, 2 * g) - lam * _normalised(acc_scr, 2 * g + 1)
            o = _rms(ot.T, gsub_ref[...]) * out_scale
            o_ref[pl.ds(pl.multiple_of(qi * t, t), t), g * LANES:(g + 1) * LANES] = o.astype(o_ref.dtype)
        return carry

    lax.fori_loop(0, n_tiles, q_tile, 0)


def _head_group_spec(s, heads, single_buffered=False):
    mode = dict(pipeline_mode=pl.Buffered(1)) if single_buffered else {}
    return pl.BlockSpec((None, s, heads * LANES), lambda bi, h: (bi, 0, h), **mode)


def _diff_attention(lam, q, k, v, g_sub, out_scale):
    b, s, w = q.shape
    n_tiles = s // ATT_TILE
    hg = DIFF_HEADS
    tiles_t = pltpu.VMEM((hg, n_tiles, LANES, ATT_TILE), BF16)
    values_t = pltpu.VMEM((hg, n_tiles, ATT_ACC_ROWS, ATT_TILE), BF16)
    parking = pltpu.VMEM((2 * hg, ATT_TILE, ATT_TILE), F32)
    return pl.pallas_call(
        functools.partial(_diff_attn_kernel, out_scale=out_scale),
        grid=(b, w // (hg * LANES)),
        in_specs=[pl.BlockSpec(memory_space=pltpu.SMEM)] + [_head_group_spec(s, hg, True)] * 3
        + [pl.BlockSpec((1, LANES), lambda bi, h: (0, 0))],
        out_specs=_head_group_spec(s, hg),
        out_shape=jax.ShapeDtypeStruct((b, s, w), BF16),
        scratch_shapes=[tiles_t, tiles_t, values_t, pltpu.VMEM((2 * hg, ATT_ACC_ROWS, ATT_TILE), F32),
                        parking, parking],
        compiler_params=_cparams("parallel", "parallel"),
        name="diff_attn",
    )(lam, q, k, v, g_sub)


def _split_hi_lo(x):
    hi = x.astype(BF16)
    return hi, (x - hi.astype(F32)).astype(BF16)


def _topk_bias(g, n_valid):
    nb = g.shape[0]
    idx = lax.broadcasted_iota(jnp.int32, g.shape, 0)
    rank = jnp.zeros(g.shape, jnp.int32)
    for m in range(nb):
        gm = g[m:m + 1, :]
        beats = jnp.where(gm > g, 1, jnp.where((gm == g) & (idx > m), 1, 0))
        rank = rank + jnp.where(n_valid > m, beats, 0)
    sel = (idx < n_valid) & (rank < MOBA_TOPK)
    return jnp.where(sel, 0.0, NEG_INF)


def _moba_kernel(q_ref, k_ref, v_ref, o_ref, qt_scr, vt_scr, kmh_scr, kml_scr, bias_scr, acc_scr,
                 park_a, park_b):
    t = ATT_TILE
    n_tiles = q_ref.shape[0] // t
    heads = range(MOBA_HEADS)
    for g in heads:
        _stage_transposed(q_ref, g, qt_scr, n_tiles)
        _stage_transposed(v_ref, g, vt_scr, n_tiles)
        k = k_ref[:, g * LANES:(g + 1) * LANES].astype(F32)
        kmh_scr[g], kml_scr[g] = _split_hi_lo(jnp.mean(k.reshape(n_tiles, t, LANES), axis=1))
    causal = (lax.broadcasted_iota(jnp.int32, (t, t), 0) <= lax.broadcasted_iota(jnp.int32, (t, t), 1))

    def q_tile(qi, carry):
        for g in heads:
            qt = qt_scr[g, qi]
            gate = (jnp.dot(kmh_scr[g], qt, preferred_element_type=F32)
                    + jnp.dot(kml_scr[g], qt, preferred_element_type=F32))
            bias = _topk_bias(gate, qi)
            for n in range(n_tiles):
                bias_scr[g, n] = bias[n:n + 1, :]

        def scores(j, diagonal):
            out = []
            for g in heads:
                s = jnp.dot(_key_block(k_ref, g, j), qt_scr[g, qi], preferred_element_type=F32)
                out.append(jnp.where(causal, s, NEG_INF) if diagonal else s + bias_scr[g, j])
            return out

        _causal_blocks(qi, MOBA_HEADS, acc_scr, park_a, park_b, scores, lambda c, j: vt_scr[c, j])
        for g in heads:
            o = _normalised(acc_scr, g).T
            o_ref[pl.ds(pl.multiple_of(qi * t, t), t), g * LANES:(g + 1) * LANES] = o.astype(o_ref.dtype)
        return carry

    lax.fori_loop(0, n_tiles, q_tile, 0)


def _moba_attention(q, k, v):
    b, s, w = q.shape
    n_tiles = s // ATT_TILE
    hg = MOBA_HEADS
    tiles_t = pltpu.VMEM((hg, n_tiles, LANES, ATT_TILE), BF16)
    values_t = pltpu.VMEM((hg, n_tiles, ATT_ACC_ROWS, ATT_TILE), BF16)
    means = pltpu.VMEM((hg, n_tiles, LANES), BF16)
    parking = pltpu.VMEM((hg, ATT_TILE, ATT_TILE), F32)
    return pl.pallas_call(
        _moba_kernel,
        grid=(b, w // (hg * LANES)),
        in_specs=[_head_group_spec(s, hg, True)] * 3,
        out_specs=_head_group_spec(s, hg),
        out_shape=jax.ShapeDtypeStruct((b, s, w), BF16),
        scratch_shapes=[tiles_t, values_t, means, means, pltpu.VMEM((hg, n_tiles, 1, ATT_TILE), F32),
                        pltpu.VMEM((hg, ATT_ACC_ROWS, ATT_TILE), F32), parking, parking],
        compiler_params=_cparams("parallel", "parallel"),
        name="moba_attn",
    )(q, k, v)


def _block_diag_queries(q, n_rep):
    t, w = q.shape
    used = n_rep * (w // LANES) * t
    rows = -(-used // LANES) * LANES
    tiled = jnp.concatenate([q] * (rows // t), axis=0)
    r = lax.broadcasted_iota(jnp.int32, (rows, w), 0)
    c = lax.broadcasted_iota(jnp.int32, (rows, w), 1)
    head = (r // t) % (w // LANES)
    part = r // (t * (w // LANES))
    keep = (c // LANES == head) & ((c % LANES) // (LANES // n_rep) == part) & (r < used)
    return jnp.where(keep, tiled, 0.0).astype(BF16)


def _dot_nt(a, b):
    return lax.dot_general(a, b, (((1,), (1,)), ((), ())), preferred_element_type=F32)


def _head_diagonal(x, t):
    return jnp.concatenate(
        [x[h * t:(h + 1) * t, h * LANES:(h + 1) * LANES] for h in range(x.shape[1] // LANES)], axis=1)


def _load_page(ref):
    n_head = H_A
    slots = ref.shape[0] // n_head
    return jnp.concatenate([ref[pl.ds(h, slots, stride=n_head), :] for h in range(n_head)], axis=1)


def _load_block(pages, b, per_blk):
    rows = jnp.concatenate([_load_page(pages[b * per_blk + i]) for i in range(per_blk)], axis=0)
    return rows.astype(BF16), rows


def _pad_rows(x, rows):
    return jnp.concatenate([x, jnp.zeros((rows - x.shape[0], x.shape[1]), x.dtype)], axis=0)


def _topk_bias_lanes(g, n_valid):
    idx = lax.broadcasted_iota(jnp.int32, g.shape, 1)
    rank = jnp.zeros(g.shape, jnp.int32)
    for m in range(n_valid):
        gm = g[:, m:m + 1]
        rank = rank + jnp.where(gm > g, 1, jnp.where((gm == g) & (idx > m), 1, 0))
    sel = (idx < n_valid) & (rank < MOBA_TOPK)
    return jnp.where(sel, 0.0, NEG_INF)


def _paged_attention(qbd, tq, kn_ref, vn_ref, k_pages, v_pages, s_scr, block_bias):
    page = k_pages[0].shape[0] // H_A
    per_blk = MOBA_BLOCK // page
    n_blk = len(k_pages) // per_blk
    blk_max, blk_sum = [], []
    for b in range(n_blk):
        kb16, kb = _load_block(k_pages, b, per_blk)
        if block_bias is not None:
            blk_sum.append(jnp.sum(kb, axis=0, keepdims=True))
        s = _dot_nt(qbd, kb16)
        s_scr[:, b * MOBA_BLOCK:(b + 1) * MOBA_BLOCK] = s
        blk_max.append(jnp.max(s, axis=1, keepdims=True))
    s_new = _dot_nt(qbd, _pad_rows(kn_ref[...].astype(BF16), LANES))
    key_t = lax.broadcasted_iota(jnp.int32, s_new.shape, 1)
    qry_t = lax.broadcasted_iota(jnp.int32, s_new.shape, 0) % tq
    s_new = jnp.where(key_t <= qry_t, s_new, NEG_INF)
    bias = None
    if block_bias is not None:
        bias = block_bias(jnp.concatenate(blk_sum, axis=0) * (1.0 / MOBA_BLOCK))
    m = jnp.max(s_new, axis=1, keepdims=True)
    for b in range(n_blk):
        m = jnp.maximum(m, blk_max[b] if bias is None else blk_max[b] + bias[:, b:b + 1])
    p_new = jnp.exp2(s_new - m)
    l_new = jnp.sum(p_new, axis=1, keepdims=True)
    acc = jnp.dot(p_new.astype(BF16), _pad_rows(vn_ref[...].astype(BF16), LANES), preferred_element_type=F32)
    l_run = jnp.zeros((qbd.shape[0], MOBA_BLOCK), F32)
    for b in range(n_blk):
        s = s_scr[:, b * MOBA_BLOCK:(b + 1) * MOBA_BLOCK]
        if bias is not None:
            s = s + bias[:, b:b + 1]
        p = jnp.exp2(s - m)
        l_run = l_run + p
        acc = acc + jnp.dot(p.astype(BF16), _load_block(v_pages, b, per_blk)[0], preferred_element_type=F32)
    return acc / (l_new + jnp.sum(l_run, axis=1, keepdims=True))


def _sample_diff_kernel(n_pages, out_scale, pt_ref, lam_ref, q_ref, kn_ref, vn_ref, gsub_ref, *rest):
    k_pages = rest[:n_pages]
    v_pages = rest[n_pages:2 * n_pages]
    o_ref, s_scr = rest[2 * n_pages], rest[2 * n_pages + 1]
    tq = q_ref.shape[0]
    qbd = _block_diag_queries(q_ref[...].astype(F32), 2)
    acc = _paged_attention(qbd, tq, kn_ref, vn_ref, k_pages, v_pages, s_scr, None)
    half = (q_ref.shape[1] // LANES) * tq
    o = _head_diagonal(acc[:half] - lam_ref[0, 0] * acc[half:2 * half], tq)
    g = gsub_ref[...]
    for h in range(o.shape[1] // LANES):
        cols = slice(h * LANES, (h + 1) * LANES)
        o_ref[:, cols] = (_rms(o[:, cols], g) * out_scale).astype(o_ref.dtype)


def _sample_moba_kernel(n_pages, pt_ref, q_ref, kn_ref, vn_ref, *rest):
    k_pages = rest[:n_pages]
    v_pages = rest[n_pages:2 * n_pages]
    o_ref, s_scr = rest[2 * n_pages], rest[2 * n_pages + 1]
    tq = q_ref.shape[0]
    qbd = _block_diag_queries(q_ref[...].astype(F32), 1)

    def block_bias(means):
        n_blk = means.shape[0]
        km_hi, km_lo = _split_hi_lo(_pad_rows(means, -(-n_blk // 16) * 16))
        return _topk_bias_lanes(_dot_nt(qbd, km_hi) + _dot_nt(qbd, km_lo), n_blk)

    acc = _paged_attention(qbd, tq, kn_ref, vn_ref, k_pages, v_pages, s_scr, block_bias)
    o_ref[...] = _head_diagonal(acc, tq).astype(o_ref.dtype)


def _sample_attention(kernel_fn, name, page_table, scalars, q, k_new, v_new, extra, cache_k, cache_v, n_col):
    db, tq, w = q.shape
    n_pages = page_table.shape[1]
    page_rows = cache_k.shape[1]
    page = page_rows // (w // LANES)
    tok = lambda: pl.BlockSpec((None, tq, w), lambda b, pt: (b, 0, 0))
    page_spec = lambda p: pl.BlockSpec((None, page_rows, LANES), lambda b, pt: (pt[b * n_pages + p], 0, 0))
    in_specs = ([pl.BlockSpec(memory_space=pltpu.SMEM) for _ in scalars] + [tok(), tok(), tok()]
                + [pl.BlockSpec((1, LANES), lambda b, pt: (0, 0)) for _ in extra]
                + [page_spec(p) for p in range(n_pages)] * 2)
    return pl.pallas_call(
        kernel_fn,
        grid_spec=pltpu.PrefetchScalarGridSpec(
            num_scalar_prefetch=1, grid=(db,), in_specs=in_specs, out_specs=tok(),
            scratch_shapes=[pltpu.VMEM((n_col, n_pages * page), F32)]),
        out_shape=jax.ShapeDtypeStruct((db, tq, w), BF16),
        compiler_params=_cparams("arbitrary"),
        name=name,
    )(page_table.reshape(-1), *scalars, q, k_new, v_new, *extra, *([cache_k] * n_pages), *([cache_v] * n_pages))


def _post_kernel(x_ref, oa_ref, ob_ref, pe_ref, gmix_ref, wg_ref, wa_ref, wb_ref, wo_ref, gmlp_ref,
                 wup_ref, wdn_ref, gple_ref, wpg_ref, wpp_ref, y_ref):
    dot = functools.partial(jnp.dot, preferred_element_type=F32)
    x = x_ref[...]
    d = x.shape[1]
    xn = _rms(x, gmix_ref[...]).astype(BF16)
    merged = (jax.nn.sigmoid(dot(xn, wg_ref[:, :d])) * dot(oa_ref[...], wa_ref[...])
              + jax.nn.sigmoid(dot(xn, wg_ref[:, d:])) * dot(ob_ref[...], wb_ref[...]))
    h = x + dot(merged.astype(BF16), wo_ref[...])
    u = dot(_rms(h, gmlp_ref[...]).astype(BF16), wup_ref[...])
    h = h + dot(jnp.square(jnp.maximum(u, 0.0)).astype(BF16), wdn_ref[...])
    gate = jax.nn.sigmoid(dot(_rms(h, gple_ref[...]).astype(BF16), wpg_ref[...]))
    y_ref[...] = h + gate * dot(pe_ref[...].astype(BF16), wpp_ref[...])


def _post(x, oa, ob, pe, g_mix, w_in16, weights, tm):
    m, d = x.shape
    row = lambda a: pl.BlockSpec((tm, a.shape[1]), lambda i: (i, 0))
    whole = lambda a: pl.BlockSpec(a.shape, lambda i: (0, 0), pipeline_mode=pl.Buffered(1))
    gate_w = pl.BlockSpec((d, 2 * d), lambda i: (0, N_QKV_SEG // 2), pipeline_mode=pl.Buffered(1))
    acts = (x, oa, ob, pe)
    return pl.pallas_call(
        _post_kernel, grid=(m // tm,),
        in_specs=[row(a) for a in acts] + [whole(g_mix), gate_w] + [whole(a) for a in weights],
        out_specs=pl.BlockSpec((tm, d), lambda i: (i, 0)),
        out_shape=jax.ShapeDtypeStruct((m, d), F32),
        compiler_params=_cparams("parallel"),
        name="post",
    )(*acts, g_mix, w_in16, *weights)


def _row_tile(m, cap):
    t = min(m, cap)
    assert m % t == 0, (m, t)
    return t


def kernel(x_prompt, x_sample, p_prompt, p_sample, cache_a_k, cache_a_v, cache_b_k, cache_b_v, page_table,
           g_mix, w_in, qn_a, kn_a, lam_q1, lam_k1, lam_q2, lam_k2, g_sub_a, qn_b, kn_b,
           w_br_a, w_br_b, w_o, g_mlp, w_up, w_down, g_ple, w_ple_gate, w_ple_proj):
    b, s, d = x_prompt.shape
    db, t, _ = x_sample.shape
    depth = g_mix.shape[0]
    n_pool, page = cache_a_k.shape[1], cache_a_k.shape[2]
    n_pages = page_table.shape[1]
    past = n_pages * page
    w_a, w_b = H_A * 2 * D_A, H_B * D_B
    assert s % ATT_TILE == 0 and ATT_TILE == MOBA_BLOCK and w_a == w_b == d and H_A == H_B
    assert past % MOBA_BLOCK == 0 and MOBA_BLOCK % page == 0 and t <= MOBA_BLOCK and past // MOBA_BLOCK >= MOBA_TOPK
    assert w_in.shape[2] == N_SEG * d

    pos_p = jnp.arange(s, dtype=jnp.int32)
    pos_s = past + jnp.arange(t, dtype=jnp.int32)
    tm_p = _row_tile(s, PROJ_ROWS)
    tm_s = _row_tile(db * t, PROJ_ROWS)
    assert tm_s % t == 0
    tab_p = _rope_tables(pos_p, D_A) + _rope_tables(pos_p, D_B)
    tab_s = tuple(jnp.tile(a, (tm_s // t, 1)) for dd in (D_A, D_B) for a in _rope_tables(pos_s, dd))

    hp = x_prompt.reshape(b * s, d)
    hs = x_sample.reshape(db * t, d)
    caches = ([], [], [], [], [], [], [], [])
    tile2 = lambda g: jnp.tile(g, 2)[None, :]
    for l in range(depth):
        lam_init = 0.8 - 0.6 * math.exp(-0.3 * l)
        lam = (jnp.exp(jnp.sum(lam_q1[l] * lam_k1[l])) - jnp.exp(jnp.sum(lam_q2[l] * lam_k2[l])) + lam_init)
        lam = lam.reshape(1, 1).astype(F32)
        out_scale = 1.0 - lam_init
        g_mix_l = g_mix[l][None, :]
        w_in16 = w_in[l].astype(BF16)
        gains = (tile2(qn_a[l]) * (D_A ** -0.5 * LOG2_E), tile2(kn_a[l]),
                 qn_b[l][None, :] * (D_B ** -0.5 * LOG2_E), kn_b[l][None, :])
        post_w = (w_br_a[l].astype(BF16), w_br_b[l].astype(BF16), w_o[l].astype(BF16), g_mlp[l][None, :],
                  w_up[l].astype(BF16), w_down[l].astype(BF16), g_ple[l][None, :],
                  w_ple_gate[l].astype(BF16), w_ple_proj[l].astype(BF16))
        g_sub = g_sub_a[l][None, :]

        qa, ka, va, qb, kb, vb, ka16, va16, kb16, vb16 = _project(hp, g_mix_l, w_in16, gains, tab_p, tm_p)
        r3 = lambda a: a.reshape(b, s, d)
        oa = _diff_attention(lam, r3(qa), r3(ka16), r3(va16), g_sub, out_scale).reshape(b * s, d)
        ob = _moba_attention(r3(qb), r3(kb16), r3(vb16)).reshape(b * s, d)
        hp = _post(hp, oa, ob, p_prompt[l].reshape(b * s, -1), g_mix_l, w_in16, post_w,
                   _row_tile(b * s, POST_ROWS))
        for dst, a in zip(caches[:4], (ka, va, kb, vb)):
            dst.append(a.reshape(b, s, H_A, LANES))

        qa, ka, va, qb, kb, vb = _project(hs, g_mix_l, w_in16, gains, tab_s, tm_s)[:6]
        r3 = lambda a: a.reshape(db, t, d)
        c3 = lambda c: c[l].reshape(n_pool, page * H_A, LANES)
        n_col = -(-2 * H_A * t // LANES) * LANES
        oa = _sample_attention(functools.partial(_sample_diff_kernel, n_pages, out_scale), "sample_diff_attn",
                               page_table, (lam,), r3(qa), r3(ka), r3(va), (g_sub,),
                               c3(cache_a_k), c3(cache_a_v), n_col)
        n_col = -(-H_B * t // LANES) * LANES
        ob = _sample_attention(functools.partial(_sample_moba_kernel, n_pages), "sample_moba_attn",
                               page_table, (), r3(qb), r3(kb), r3(vb), (), c3(cache_b_k), c3(cache_b_v), n_col)
        hs = _post(hs, oa.reshape(db * t, d), ob.reshape(db * t, d), p_sample[l].reshape(db * t, -1),
                   g_mix_l, w_in16, post_w, _row_tile(db * t, POST_ROWS))
        for dst, a in zip(caches[4:], (ka, va, kb, vb)):
            dst.append(a.reshape(db, t, H_A, LANES))

    return (hp.reshape(b, s, d), hs.reshape(db, t, d)) + tuple(jnp.stack(c) for c in caches)
```

```python
import functools
import math

import jax
import jax.numpy as jnp
from jax import lax
from jax.experimental import pallas as pl
from jax.experimental.pallas import tpu as pltpu

F32 = jnp.float32
BF16 = jnp.bfloat16

LANES = 128
H_A = 8
D_A = 64
H_B = 8
D_B = 128
MOBA_BLOCK = 256
MOBA_TOPK = 3
ROPE_THETA = 10000.0
EPS = 1e-6
N_SEG = 8
N_QKV_SEG = 6
NEG_INF = float("-inf")
VMEM_LIMIT = 56 * 1024 * 1024
ATT_TILE = 256
DIFF_HEADS = 4
MOBA_HEADS = 4
ATT_ACC_ROWS = LANES + 16
LOG2_E = math.log2(math.e)
PROJ_ROWS = 512
POST_ROWS = 256


def _cparams(*sem):
    return pltpu.CompilerParams(dimension_semantics=sem, vmem_limit_bytes=VMEM_LIMIT)


def _rms(x, g):
    return x * lax.rsqrt(jnp.mean(x * x, axis=-1, keepdims=True) + EPS) * g


def _norm_rope(zh, g, c, s, group):
    sq = zh * zh
    lane = lax.broadcasted_iota(jnp.int32, zh.shape, 1)
    if group == LANES:
        ss = jnp.sum(sq, axis=-1, keepdims=True)
    else:
        lo = lane < group
        s_lo = jnp.sum(jnp.where(lo, sq, 0.0), axis=-1, keepdims=True)
        s_all = jnp.sum(sq, axis=-1, keepdims=True)
        ss = jnp.where(lo, s_lo, s_all - s_lo)
    y = zh * lax.rsqrt(ss * (1.0 / group) + EPS) * g
    half = group // 2
    if group == LANES:
        sw = pltpu.roll(y, half, axis=1)
    else:
        first = (lane % group) < half
        sw = jnp.where(first, pltpu.roll(y, LANES - half, axis=1), pltpu.roll(y, half, axis=1))
    return y * c + sw * s


def _proj_kernel(x_ref, gmix_ref, w_ref, gqa_ref, gka_ref, gqb_ref, gkb_ref, ca_ref, sa_ref, cb_ref, sb_ref,
                 qa_ref, ka_ref, va_ref, qb_ref, kb_ref, vb_ref, ka16_ref, va16_ref, kb16_ref, vb16_ref):
    xn = _rms(x_ref[...], gmix_ref[...]).astype(BF16)
    width = qa_ref.shape[1]

    def segment(i):
        return jnp.dot(xn, w_ref[:, i * width:(i + 1) * width], preferred_element_type=F32)

    def store_rope(o_refs, z, group, g_ref, c_ref, s_ref):
        g, c, s = g_ref[...], c_ref[...], s_ref[...]
        for h in range(width // LANES):
            cols = slice(h * LANES, (h + 1) * LANES)
            y = _norm_rope(z[:, cols], g, c, s, group)
            for o_ref in o_refs:
                o_ref[:, cols] = y.astype(o_ref.dtype)

    def store_plain(o_refs, z):
        for o_ref in o_refs:
            o_ref[...] = z.astype(o_ref.dtype)

    store_rope((qa_ref,), segment(0), D_A, gqa_ref, ca_ref, sa_ref)
    store_rope((ka_ref, ka16_ref), segment(1), D_A, gka_ref, ca_ref, sa_ref)
    store_plain((va_ref, va16_ref), segment(2))
    store_rope((qb_ref,), segment(3), D_B, gqb_ref, cb_ref, sb_ref)
    store_rope((kb_ref, kb16_ref), segment(4), D_B, gkb_ref, cb_ref, sb_ref)
    store_plain((vb_ref, vb16_ref), segment(5))


def _project(x, g_mix, w_in16, gains, tables, tm):
    m, d = x.shape
    width = w_in16.shape[1] // N_SEG
    n_tab = tables[0].shape[0] // tm
    row = lambda w: pl.BlockSpec((tm, w), lambda i: (i, 0))
    const = lambda a: pl.BlockSpec(a.shape, lambda i: (0, 0), pipeline_mode=pl.Buffered(1))
    w_spec = pl.BlockSpec((d, N_QKV_SEG * width), lambda i: (0, 0), pipeline_mode=pl.Buffered(1))
    tab = pl.BlockSpec((tm, LANES), lambda i: (i % n_tab, 0))
    out_dtypes = (BF16, F32, F32, BF16, F32, F32, BF16, BF16, BF16, BF16)
    return pl.pallas_call(
        _proj_kernel, grid=(m // tm,),
        in_specs=[row(d), const(g_mix), w_spec] + [const(g) for g in gains] + [tab] * 4,
        out_specs=[row(width)] * len(out_dtypes),
        out_shape=[jax.ShapeDtypeStruct((m, width), dt) for dt in out_dtypes],
        compiler_params=_cparams("parallel"),
        name="proj",
    )(x, g_mix, w_in16, *gains, *tables)


def _rope_tables(pos, d):
    half = d // 2
    inv = ROPE_THETA ** (-jnp.arange(half, dtype=F32) * 2.0 / d)
    ang = pos.astype(F32)[:, None] * inv[None, :]
    cos, sin = jnp.cos(ang), jnp.sin(ang)
    reps = LANES // d
    c = jnp.tile(jnp.concatenate([cos, cos], axis=-1), (1, reps))
    s = jnp.tile(jnp.concatenate([-sin, sin], axis=-1), (1, reps))
    return c, s


def _stage_transposed(src_ref, g, dst_scr, n_tiles, row_mask=None):
    cols = slice(g * LANES, (g + 1) * LANES)
    extra = dst_scr.shape[2] - LANES
    for i in range(n_tiles):
        t = src_ref[i * ATT_TILE:(i + 1) * ATT_TILE, cols].astype(F32).T
        if row_mask is not None:
            t = jnp.where(row_mask, t, 0.0)
        dst_scr[g, i, 0:LANES, :] = t.astype(BF16)
        if extra:
            dst_scr[g, i, LANES:LANES + extra, :] = jnp.ones((extra, ATT_TILE), BF16)


def _park_scores(dst_scr, tiles):
    for c, s in enumerate(tiles):
        dst_scr[c] = s


def _causal_blocks(qi, n_tiles, n_chains, acc_scr, park_a, park_b, scores, values_t):
    t = ATT_TILE
    chains = range(n_chains)

    def park(j, dst_scr, diagonal):
        _park_scores(dst_scr, scores(qi, j, diagonal))

    def park_next_tile():
        nxt = jnp.minimum(qi + 1, n_tiles - 1)
        _park_scores(park_a, scores(nxt, nxt, True))

    def consume(j, src_scr, maxes):
        out, pend = [], []
        for c in chains:
            s = src_scr[c]
            m_new = jnp.maximum(maxes[c], jnp.max(s, axis=0, keepdims=True))
            p = jnp.exp2(s - m_new).astype(BF16)
            pend.append((jnp.exp2(maxes[c] - m_new), jnp.dot(values_t(c, j), p, preferred_element_type=F32)))
            out.append(m_new)
        return tuple(out), pend

    def accumulate(*pending):
        for c in chains:
            acc = acc_scr[c]
            for pend in pending:
                a, pv = pend[c]
                acc = acc * a + pv
            acc_scr[c] = acc

    acc_scr[...] = jnp.zeros_like(acc_scr)
    init = (jnp.full((1, t), NEG_INF, F32),) * n_chains

    def pair(pi, stats):
        j = 2 * pi
        park(j, park_b, False)
        stats, pend_a = consume(jnp.where(pi == 0, qi, j - 1), park_a, stats)
        park(j + 1, park_a, False)
        stats, pend_b = consume(j, park_b, stats)
        accumulate(pend_a, pend_b)
        return stats

    n_pairs = qi // 2
    stats = lax.fori_loop(0, n_pairs, pair, init)
    in_a = jnp.where(n_pairs == 0, qi, 2 * n_pairs - 1)

    def odd_tail(stats):
        park(qi - 1, park_b, False)
        stats, pend_a = consume(in_a, park_a, stats)
        stats, pend_b = consume(qi - 1, park_b, stats)
        accumulate(pend_a, pend_b)
        return stats

    def even_tail(stats):
        stats, pend_a = consume(in_a, park_a, stats)
        accumulate(pend_a)
        return stats

    lax.cond(qi % 2 == 1, odd_tail, even_tail, stats)
    park_next_tile()


def _normalised(acc_scr, c):
    return acc_scr[c, 0:LANES, :] / acc_scr[c, LANES:LANES + 1, :]


def _key_block(k_ref, g, j):
    return k_ref[pl.ds(pl.multiple_of(j * ATT_TILE, ATT_TILE), ATT_TILE), g * LANES:(g + 1) * LANES]


def _diff_attn_kernel(lam_ref, q_ref, k_ref, v_ref, gsub_ref, o_ref,
                      q1t_scr, q2t_scr, vt_scr, acc_scr, park_a, park_b, *, out_scale):
    t = ATT_TILE
    n_tiles = q_ref.shape[0] // t
    heads = range(DIFF_HEADS)
    row = lax.broadcasted_iota(jnp.int32, (LANES, t), 0)
    for g in heads:
        _stage_transposed(q_ref, g, q1t_scr, n_tiles, row < D_A)
        _stage_transposed(q_ref, g, q2t_scr, n_tiles, row >= D_A)
        _stage_transposed(v_ref, g, vt_scr, n_tiles)
    lam = lam_ref[0, 0]
    causal = (lax.broadcasted_iota(jnp.int32, (t, t), 0) <= lax.broadcasted_iota(jnp.int32, (t, t), 1))

    def scores(q_tile, j, diagonal):
        out = []
        for g in heads:
            kj = _key_block(k_ref, g, j)
            for qt_scr in (q1t_scr, q2t_scr):
                s = jnp.dot(kj, qt_scr[g, q_tile], preferred_element_type=F32)
                out.append(jnp.where(causal, s, NEG_INF) if diagonal else s)
        return out

    _park_scores(park_a, scores(0, 0, True))

    def q_tile(qi, carry):
        _causal_blocks(qi, n_tiles, 2 * DIFF_HEADS, acc_scr, park_a, park_b, scores,
                       lambda c, j: vt_scr[c // 2, j])
        for g in heads:
            ot = _normalised(acc_scr, 2 * g) - lam * _normalised(acc_scr, 2 * g + 1)
            o = _rms(ot.T, gsub_ref[...]) * out_scale
            o_ref[pl.ds(pl.multiple_of(qi * t, t), t), g * LANES:(g + 1) * LANES] = o.astype(o_ref.dtype)
        return carry

    lax.fori_loop(0, n_tiles, q_tile, 0)


def _head_group_spec(s, heads, single_buffered=False):
    mode = dict(pipeline_mode=pl.Buffered(1)) if single_buffered else {}
    return pl.BlockSpec((None, s, heads * LANES), lambda bi, h: (bi, 0, h), **mode)


def _diff_attention(lam, q, k, v, g_sub, out_scale):
    b, s, w = q.shape
    n_tiles = s // ATT_TILE
    hg = DIFF_HEADS
    tiles_t = pltpu.VMEM((hg, n_tiles, LANES, ATT_TILE), BF16)
    values_t = pltpu.VMEM((hg, n_tiles, ATT_ACC_ROWS, ATT_TILE), BF16)
    parking = pltpu.VMEM((2 * hg, ATT_TILE, ATT_TILE), F32)
    return pl.pallas_call(
        functools.partial(_diff_attn_kernel, out_scale=out_scale),
        grid=(b, w // (hg * LANES)),
        in_specs=[pl.BlockSpec(memory_space=pltpu.SMEM)] + [_head_group_spec(s, hg)] * 3
        + [pl.BlockSpec((1, LANES), lambda bi, h: (0, 0))],
        out_specs=_head_group_spec(s, hg),
        out_shape=jax.ShapeDtypeStruct((b, s, w), BF16),
        scratch_shapes=[tiles_t, tiles_t, values_t, pltpu.VMEM((2 * hg, ATT_ACC_ROWS, ATT_TILE), F32),
                        parking, parking],
        compiler_params=_cparams("parallel", "parallel"),
        name="diff_attn",
    )(lam, q, k, v, g_sub)


def _split_hi_lo(x):
    hi = x.astype(BF16)
    return hi, (x - hi.astype(F32)).astype(BF16)


def _topk_bias(g, n_valid):
    nb = g.shape[0]
    idx = lax.broadcasted_iota(jnp.int32, g.shape, 0)
    rank = jnp.zeros(g.shape, jnp.int32)
    for m in range(nb):
        gm = g[m:m + 1, :]
        beats = jnp.where(gm > g, 1, jnp.where((gm == g) & (idx > m), 1, 0))
        rank = rank + jnp.where(n_valid > m, beats, 0)
    sel = (idx < n_valid) & (rank < MOBA_TOPK)
    return jnp.where(sel, 0.0, NEG_INF)


def _moba_kernel(q_ref, k_ref, v_ref, o_ref, qt_scr, vt_scr, kmh_scr, kml_scr, bias_scr, acc_scr,
                 park_a, park_b):
    t = ATT_TILE
    n_tiles = q_ref.shape[0] // t
    heads = range(MOBA_HEADS)
    for g in heads:
        _stage_transposed(q_ref, g, qt_scr, n_tiles)
        _stage_transposed(v_ref, g, vt_scr, n_tiles)
        k = k_ref[:, g * LANES:(g + 1) * LANES].astype(F32)
        kmh_scr[g], kml_scr[g] = _split_hi_lo(jnp.mean(k.reshape(n_tiles, t, LANES), axis=1))
    causal = (lax.broadcasted_iota(jnp.int32, (t, t), 0) <= lax.broadcasted_iota(jnp.int32, (t, t), 1))

    def scores(q_tile, j, diagonal):
        out = []
        for g in heads:
            s = jnp.dot(_key_block(k_ref, g, j), qt_scr[g, q_tile], preferred_element_type=F32)
            out.append(jnp.where(causal, s, NEG_INF) if diagonal else s + bias_scr[g, j])
        return out

    _park_scores(park_a, scores(0, 0, True))

    def q_tile(qi, carry):
        for g in heads:
            qt = qt_scr[g, qi]
            gate = (jnp.dot(kmh_scr[g], qt, preferred_element_type=F32)
                    + jnp.dot(kml_scr[g], qt, preferred_element_type=F32))
            bias = _topk_bias(gate, qi)
            for n in range(n_tiles):
                bias_scr[g, n] = bias[n:n + 1, :]

        _causal_blocks(qi, n_tiles, MOBA_HEADS, acc_scr, park_a, park_b, scores, lambda c, j: vt_scr[c, j])
        for g in heads:
            o = _normalised(acc_scr, g).T
            o_ref[pl.ds(pl.multiple_of(qi * t, t), t), g * LANES:(g + 1) * LANES] = o.astype(o_ref.dtype)
        return carry

    lax.fori_loop(0, n_tiles, q_tile, 0)


def _moba_attention(q, k, v):
    b, s, w = q.shape
    n_tiles = s // ATT_TILE
    hg = MOBA_HEADS
    tiles_t = pltpu.VMEM((hg, n_tiles, LANES, ATT_TILE), BF16)
    values_t = pltpu.VMEM((hg, n_tiles, ATT_ACC_ROWS, ATT_TILE), BF16)
    means = pltpu.VMEM((hg, n_tiles, LANES), BF16)
    parking = pltpu.VMEM((hg, ATT_TILE, ATT_TILE), F32)
    return pl.pallas_call(
        _moba_kernel,
        grid=(b, w // (hg * LANES)),
        in_specs=[_head_group_spec(s, hg)] * 3,
        out_specs=_head_group_spec(s, hg),
        out_shape=jax.ShapeDtypeStruct((b, s, w), BF16),
        scratch_shapes=[tiles_t, values_t, means, means, pltpu.VMEM((hg, n_tiles, 1, ATT_TILE), F32),
                        pltpu.VMEM((hg, ATT_ACC_ROWS, ATT_TILE), F32), parking, parking],
        compiler_params=_cparams("parallel", "parallel"),
        name="moba_attn",
    )(q, k, v)


def _block_diag_queries(q, n_rep):
    t, w = q.shape
    used = n_rep * (w // LANES) * t
    rows = -(-used // LANES) * LANES
    tiled = jnp.concatenate([q] * (rows // t), axis=0)
    r = lax.broadcasted_iota(jnp.int32, (rows, w), 0)
    c = lax.broadcasted_iota(jnp.int32, (rows, w), 1)
    head = (r // t) % (w // LANES)
    part = r // (t * (w // LANES))
    keep = (c // LANES == head) & ((c % LANES) // (LANES // n_rep) == part) & (r < used)
    return jnp.where(keep, tiled, 0.0).astype(BF16)


def _dot_nt(a, b):
    return lax.dot_general(a, b, (((1,), (1,)), ((), ())), preferred_element_type=F32)


def _head_diagonal(x, t):
    return jnp.concatenate(
        [x[h * t:(h + 1) * t, h * LANES:(h + 1) * LANES] for h in range(x.shape[1] // LANES)], axis=1)


def _load_page(ref):
    n_head = H_A
    slots = ref.shape[0] // n_head
    return jnp.concatenate([ref[pl.ds(h, slots, stride=n_head), :] for h in range(n_head)], axis=1)


def _load_block(pages, b, per_blk):
    rows = jnp.concatenate([_load_page(pages[b * per_blk + i]) for i in range(per_blk)], axis=0)
    return rows.astype(BF16), rows


def _pad_rows(x, rows):
    return jnp.concatenate([x, jnp.zeros((rows - x.shape[0], x.shape[1]), x.dtype)], axis=0)


def _topk_bias_lanes(g, n_valid):
    idx = lax.broadcasted_iota(jnp.int32, g.shape, 1)
    rank = jnp.zeros(g.shape, jnp.int32)
    for m in range(n_valid):
        gm = g[:, m:m + 1]
        rank = rank + jnp.where(gm > g, 1, jnp.where((gm == g) & (idx > m), 1, 0))
    sel = (idx < n_valid) & (rank < MOBA_TOPK)
    return jnp.where(sel, 0.0, NEG_INF)


def _paged_attention(qbd, tq, kn_ref, vn_ref, k_pages, v_pages, s_scr, block_bias):
    page = k_pages[0].shape[0] // H_A
    per_blk = MOBA_BLOCK // page
    n_blk = len(k_pages) // per_blk
    blk_max, blk_sum = [], []
    for b in range(n_blk):
        kb16, kb = _load_block(k_pages, b, per_blk)
        if block_bias is not None:
            blk_sum.append(jnp.sum(kb, axis=0, keepdims=True))
        s = _dot_nt(qbd, kb16)
        s_scr[:, b * MOBA_BLOCK:(b + 1) * MOBA_BLOCK] = s
        blk_max.append(jnp.max(s, axis=1, keepdims=True))
    s_new = _dot_nt(qbd, _pad_rows(kn_ref[...].astype(BF16), LANES))
    key_t = lax.broadcasted_iota(jnp.int32, s_new.shape, 1)
    qry_t = lax.broadcasted_iota(jnp.int32, s_new.shape, 0) % tq
    s_new = jnp.where(key_t <= qry_t, s_new, NEG_INF)
    bias = None
    if block_bias is not None:
        bias = block_bias(jnp.concatenate(blk_sum, axis=0) * (1.0 / MOBA_BLOCK))
    m = jnp.max(s_new, axis=1, keepdims=True)
    for b in range(n_blk):
        m = jnp.maximum(m, blk_max[b] if bias is None else blk_max[b] + bias[:, b:b + 1])
    p_new = jnp.exp2(s_new - m)
    l_new = jnp.sum(p_new, axis=1, keepdims=True)
    acc = jnp.dot(p_new.astype(BF16), _pad_rows(vn_ref[...].astype(BF16), LANES), preferred_element_type=F32)
    l_run = jnp.zeros((qbd.shape[0], MOBA_BLOCK), F32)
    for b in range(n_blk):
        s = s_scr[:, b * MOBA_BLOCK:(b + 1) * MOBA_BLOCK]
        if bias is not None:
            s = s + bias[:, b:b + 1]
        p = jnp.exp2(s - m)
        l_run = l_run + p
        acc = acc + jnp.dot(p.astype(BF16), _load_block(v_pages, b, per_blk)[0], preferred_element_type=F32)
    return acc / (l_new + jnp.sum(l_run, axis=1, keepdims=True))


def _sample_diff_kernel(n_pages, out_scale, pt_ref, lam_ref, q_ref, kn_ref, vn_ref, gsub_ref, *rest):
    k_pages = rest[:n_pages]
    v_pages = rest[n_pages:2 * n_pages]
    o_ref, s_scr = rest[2 * n_pages], rest[2 * n_pages + 1]
    tq = q_ref.shape[0]
    qbd = _block_diag_queries(q_ref[...].astype(F32), 2)
    acc = _paged_attention(qbd, tq, kn_ref, vn_ref, k_pages, v_pages, s_scr, None)
    half = (q_ref.shape[1] // LANES) * tq
    o = _head_diagonal(acc[:half] - lam_ref[0, 0] * acc[half:2 * half], tq)
    g = gsub_ref[...]
    for h in range(o.shape[1] // LANES):
        cols = slice(h * LANES, (h + 1) * LANES)
        o_ref[:, cols] = (_rms(o[:, cols], g) * out_scale).astype(o_ref.dtype)


def _sample_moba_kernel(n_pages, pt_ref, q_ref, kn_ref, vn_ref, *rest):
    k_pages = rest[:n_pages]
    v_pages = rest[n_pages:2 * n_pages]
    o_ref, s_scr = rest[2 * n_pages], rest[2 * n_pages + 1]
    tq = q_ref.shape[0]
    qbd = _block_diag_queries(q_ref[...].astype(F32), 1)

    def block_bias(means):
        n_blk = means.shape[0]
        km_hi, km_lo = _split_hi_lo(_pad_rows(means, -(-n_blk // 16) * 16))
        return _topk_bias_lanes(_dot_nt(qbd, km_hi) + _dot_nt(qbd, km_lo), n_blk)

    acc = _paged_attention(qbd, tq, kn_ref, vn_ref, k_pages, v_pages, s_scr, block_bias)
    o_ref[...] = _head_diagonal(acc, tq).astype(o_ref.dtype)


def _sample_attention(kernel_fn, name, page_table, scalars, q, k_new, v_new, extra, cache_k, cache_v, n_col):
    db, tq, w = q.shape
    n_pages = page_table.shape[1]
    page_rows = cache_k.shape[1]
    page = page_rows // (w // LANES)
    tok = lambda: pl.BlockSpec((None, tq, w), lambda b, pt: (b, 0, 0))
    page_spec = lambda p: pl.BlockSpec((None, page_rows, LANES), lambda b, pt: (pt[b * n_pages + p], 0, 0))
    in_specs = ([pl.BlockSpec(memory_space=pltpu.SMEM) for _ in scalars] + [tok(), tok(), tok()]
                + [pl.BlockSpec((1, LANES), lambda b, pt: (0, 0)) for _ in extra]
                + [page_spec(p) for p in range(n_pages)] * 2)
    return pl.pallas_call(
        kernel_fn,
        grid_spec=pltpu.PrefetchScalarGridSpec(
            num_scalar_prefetch=1, grid=(db,), in_specs=in_specs, out_specs=tok(),
            scratch_shapes=[pltpu.VMEM((n_col, n_pages * page), F32)]),
        out_shape=jax.ShapeDtypeStruct((db, tq, w), BF16),
        compiler_params=_cparams("arbitrary"),
        name=name,
    )(page_table.reshape(-1), *scalars, q, k_new, v_new, *extra, *([cache_k] * n_pages), *([cache_v] * n_pages))


def _post_kernel(x_ref, oa_ref, ob_ref, pe_ref, gmix_ref, wg_ref, wa_ref, wb_ref, wo_ref, gmlp_ref,
                 wup_ref, wdn_ref, gple_ref, wpg_ref, wpp_ref, y_ref):
    dot = functools.partial(jnp.dot, preferred_element_type=F32)
    x = x_ref[...]
    d = x.shape[1]
    xn = _rms(x, gmix_ref[...]).astype(BF16)
    merged = (jax.nn.sigmoid(dot(xn, wg_ref[:, :d])) * dot(oa_ref[...], wa_ref[...])
              + jax.nn.sigmoid(dot(xn, wg_ref[:, d:])) * dot(ob_ref[...], wb_ref[...]))
    h = x + dot(merged.astype(BF16), wo_ref[...])
    u = dot(_rms(h, gmlp_ref[...]).astype(BF16), wup_ref[...])
    h = h + dot(jnp.square(jnp.maximum(u, 0.0)).astype(BF16), wdn_ref[...])
    gate = jax.nn.sigmoid(dot(_rms(h, gple_ref[...]).astype(BF16), wpg_ref[...]))
    y_ref[...] = h + gate * dot(pe_ref[...].astype(BF16), wpp_ref[...])


def _post(x, oa, ob, pe, g_mix, w_in16, weights, tm):
    m, d = x.shape
    row = lambda a: pl.BlockSpec((tm, a.shape[1]), lambda i: (i, 0))
    whole = lambda a: pl.BlockSpec(a.shape, lambda i: (0, 0), pipeline_mode=pl.Buffered(1))
    gate_w = pl.BlockSpec((d, 2 * d), lambda i: (0, N_QKV_SEG // 2), pipeline_mode=pl.Buffered(1))
    acts = (x, oa, ob, pe)
    return pl.pallas_call(
        _post_kernel, grid=(m // tm,),
        in_specs=[row(a) for a in acts] + [whole(g_mix), gate_w] + [whole(a) for a in weights],
        out_specs=pl.BlockSpec((tm, d), lambda i: (i, 0)),
        out_shape=jax.ShapeDtypeStruct((m, d), F32),
        compiler_params=_cparams("parallel"),
        name="post",
    )(*acts, g_mix, w_in16, *weights)


def _row_tile(m, cap):
    t = min(m, cap)
    assert m % t == 0, (m, t)
    return t


def kernel(x_prompt, x_sample, p_prompt, p_sample, cache_a_k, cache_a_v, cache_b_k, cache_b_v, page_table,
           g_mix, w_in, qn_a, kn_a, lam_q1, lam_k1, lam_q2, lam_k2, g_sub_a, qn_b, kn_b,
           w_br_a, w_br_b, w_o, g_mlp, w_up, w_down, g_ple, w_ple_gate, w_ple_proj):
    b, s, d = x_prompt.shape
    db, t, _ = x_sample.shape
    depth = g_mix.shape[0]
    n_pool, page = cache_a_k.shape[1], cache_a_k.shape[2]
    n_pages = page_table.shape[1]
    past = n_pages * page
    w_a, w_b = H_A * 2 * D_A, H_B * D_B
    assert s % ATT_TILE == 0 and ATT_TILE == MOBA_BLOCK and w_a == w_b == d and H_A == H_B
    assert past % MOBA_BLOCK == 0 and MOBA_BLOCK % page == 0 and t <= MOBA_BLOCK and past // MOBA_BLOCK >= MOBA_TOPK
    assert w_in.shape[2] == N_SEG * d

    pos_p = jnp.arange(s, dtype=jnp.int32)
    pos_s = past + jnp.arange(t, dtype=jnp.int32)
    tm_p = _row_tile(s, PROJ_ROWS)
    tm_s = _row_tile(db * t, PROJ_ROWS)
    assert tm_s % t == 0
    tab_p = _rope_tables(pos_p, D_A) + _rope_tables(pos_p, D_B)
    tab_s = tuple(jnp.tile(a, (tm_s // t, 1)) for dd in (D_A, D_B) for a in _rope_tables(pos_s, dd))

    hp = x_prompt.reshape(b * s, d)
    hs = x_sample.reshape(db * t, d)
    caches = ([], [], [], [], [], [], [], [])
    tile2 = lambda g: jnp.tile(g, 2)[None, :]
    for l in range(depth):
        lam_init = 0.8 - 0.6 * math.exp(-0.3 * l)
        lam = (jnp.exp(jnp.sum(lam_q1[l] * lam_k1[l])) - jnp.exp(jnp.sum(lam_q2[l] * lam_k2[l])) + lam_init)
        lam = lam.reshape(1, 1).astype(F32)
        out_scale = 1.0 - lam_init
        g_mix_l = g_mix[l][None, :]
        w_in16 = w_in[l].astype(BF16)
        gains = (tile2(qn_a[l]) * (D_A ** -0.5 * LOG2_E), tile2(kn_a[l]),
                 qn_b[l][None, :] * (D_B ** -0.5 * LOG2_E), kn_b[l][None, :])
        post_w = (w_br_a[l].astype(BF16), w_br_b[l].astype(BF16), w_o[l].astype(BF16), g_mlp[l][None, :],
                  w_up[l].astype(BF16), w_down[l].astype(BF16), g_ple[l][None, :],
                  w_ple_gate[l].astype(BF16), w_ple_proj[l].astype(BF16))
        g_sub = g_sub_a[l][None, :]

        qa, ka, va, qb, kb, vb, ka16, va16, kb16, vb16 = _project(hp, g_mix_l, w_in16, gains, tab_p, tm_p)
        r3 = lambda a: a.reshape(b, s, d)
        oa = _diff_attention(lam, r3(qa), r3(ka16), r3(va16), g_sub, out_scale).reshape(b * s, d)
        ob = _moba_attention(r3(qb), r3(kb16), r3(vb16)).reshape(b * s, d)
        hp = _post(hp, oa, ob, p_prompt[l].reshape(b * s, -1), g_mix_l, w_in16, post_w,
                   _row_tile(b * s, POST_ROWS))
        for dst, a in zip(caches[:4], (ka, va, kb, vb)):
            dst.append(a.reshape(b, s, H_A, LANES))

        qa, ka, va, qb, kb, vb = _project(hs, g_mix_l, w_in16, gains, tab_s, tm_s)[:6]
        r3 = lambda a: a.reshape(db, t, d)
        c3 = lambda c: c[l].reshape(n_pool, page * H_A, LANES)
        n_col = -(-2 * H_A * t // LANES) * LANES
        oa = _sample_attention(functools.partial(_sample_diff_kernel, n_pages, out_scale), "sample_diff_attn",
                               page_table, (lam,), r3(qa), r3(ka), r3(va), (g_sub,),
                               c3(cache_a_k), c3(cache_a_v), n_col)
        n_col = -(-H_B * t // LANES) * LANES
        ob = _sample_attention(functools.partial(_sample_moba_kernel, n_pages), "sample_moba_attn",
                               page_table, (), r3(qb), r3(kb), r3(vb), (), c3(cache_b_k), c3(cache_b_v), n_col)
        hs = _post(hs, oa.reshape(db * t, d), ob.reshape(db * t, d), p_sample[l].reshape(db * t, -1),
                   g_mix_l, w_in16, post_w, _row_tile(db * t, POST_ROWS))
        for dst, a in zip(caches[4:], (ka, va, kb, vb)):
            dst.append(a.reshape(db, t, H_A, LANES))

    return (hp.reshape(b, s, d), hs.reshape(db, t, d)) + tuple(jnp.stack(c) for c in caches)
```

```python
import functools
import math

import jax
import jax.numpy as jnp
from jax import lax
from jax.experimental import pallas as pl
from jax.experimental.pallas import tpu as pltpu

F32 = jnp.float32
BF16 = jnp.bfloat16

LANES = 128
H_A = 8
D_A = 64
H_B = 8
D_B = 128
MOBA_BLOCK = 256
MOBA_TOPK = 3
ROPE_THETA = 10000.0
EPS = 1e-6
N_SEG = 8
N_QKV_SEG = 6
NEG_INF = float("-inf")
VMEM_LIMIT = 56 * 1024 * 1024
ATT_TILE = 256
DIFF_HEADS = 4
MOBA_HEADS = 4
ATT_ACC_ROWS = LANES + 16
LOG2_E = math.log2(math.e)
PROJ_ROWS = 512
POST_ROWS = 256


def _cparams(*sem):
    return pltpu.CompilerParams(dimension_semantics=sem, vmem_limit_bytes=VMEM_LIMIT)


def _rms(x, g):
    return x * lax.rsqrt(jnp.mean(x * x, axis=-1, keepdims=True) + EPS) * g


def _norm_rope(zh, g, c, s, group):
    sq = zh * zh
    lane = lax.broadcasted_iota(jnp.int32, zh.shape, 1)
    if group == LANES:
        ss = jnp.sum(sq, axis=-1, keepdims=True)
    else:
        lo = lane < group
        s_lo = jnp.sum(jnp.where(lo, sq, 0.0), axis=-1, keepdims=True)
        s_all = jnp.sum(sq, axis=-1, keepdims=True)
        ss = jnp.where(lo, s_lo, s_all - s_lo)
    y = zh * lax.rsqrt(ss * (1.0 / group) + EPS) * g
    half = group // 2
    if group == LANES:
        sw = pltpu.roll(y, half, axis=1)
    else:
        first = (lane % group) < half
        sw = jnp.where(first, pltpu.roll(y, LANES - half, axis=1), pltpu.roll(y, half, axis=1))
    return y * c + sw * s


def _proj_kernel(x_ref, gmix_ref, w_ref, gqa_ref, gka_ref, gqb_ref, gkb_ref, ca_ref, sa_ref, cb_ref, sb_ref,
                 qa_ref, ka_ref, va_ref, qb_ref, kb_ref, vb_ref, ka16_ref, va16_ref, kb16_ref, vb16_ref):
    xn = _rms(x_ref[...], gmix_ref[...]).astype(BF16)
    width = qa_ref.shape[1]

    def segment(i):
        return jnp.dot(xn, w_ref[:, i * width:(i + 1) * width], preferred_element_type=F32)

    def store_rope(o_refs, z, group, g_ref, c_ref, s_ref):
        g, c, s = g_ref[...], c_ref[...], s_ref[...]
        for h in range(width // LANES):
            cols = slice(h * LANES, (h + 1) * LANES)
            y = _norm_rope(z[:, cols], g, c, s, group)
            for o_ref in o_refs:
                o_ref[:, cols] = y.astype(o_ref.dtype)

    def store_plain(o_refs, z):
        for o_ref in o_refs:
            o_ref[...] = z.astype(o_ref.dtype)

    store_rope((qa_ref,), segment(0), D_A, gqa_ref, ca_ref, sa_ref)
    store_rope((ka_ref, ka16_ref), segment(1), D_A, gka_ref, ca_ref, sa_ref)
    store_plain((va_ref, va16_ref), segment(2))
    store_rope((qb_ref,), segment(3), D_B, gqb_ref, cb_ref, sb_ref)
    store_rope((kb_ref, kb16_ref), segment(4), D_B, gkb_ref, cb_ref, sb_ref)
    store_plain((vb_ref, vb16_ref), segment(5))


def _project(x, g_mix, w_in16, gains, tables, tm):
    m, d = x.shape
    width = w_in16.shape[1] // N_SEG
    n_tab = tables[0].shape[0] // tm
    row = lambda w: pl.BlockSpec((tm, w), lambda i: (i, 0))
    const = lambda a: pl.BlockSpec(a.shape, lambda i: (0, 0), pipeline_mode=pl.Buffered(1))
    w_spec = pl.BlockSpec((d, N_QKV_SEG * width), lambda i: (0, 0), pipeline_mode=pl.Buffered(1))
    tab = pl.BlockSpec((tm, LANES), lambda i: (i % n_tab, 0))
    out_dtypes = (BF16, F32, F32, BF16, F32, F32, BF16, BF16, BF16, BF16)
    return pl.pallas_call(
        _proj_kernel, grid=(m // tm,),
        in_specs=[row(d), const(g_mix), w_spec] + [const(g) for g in gains] + [tab] * 4,
        out_specs=[row(width)] * len(out_dtypes),
        out_shape=[jax.ShapeDtypeStruct((m, width), dt) for dt in out_dtypes],
        compiler_params=_cparams("parallel"),
        name="proj",
    )(x, g_mix, w_in16, *gains, *tables)


def _rope_tables(pos, d):
    half = d // 2
    inv = ROPE_THETA ** (-jnp.arange(half, dtype=F32) * 2.0 / d)
    lane = jnp.arange(LANES)
    ang = pos.astype(F32)[:, None] * inv[lane % half][None, :]
    first_half = ((lane % d) < half)[None, :]
    return jnp.cos(ang), jnp.where(first_half, -jnp.sin(ang), jnp.sin(ang))


def _stage_transposed(src_ref, g, dst_scr, n_tiles):
    cols = slice(g * LANES, (g + 1) * LANES)
    extra = dst_scr.shape[2] - LANES
    for i in range(n_tiles):
        t = src_ref[i * ATT_TILE:(i + 1) * ATT_TILE, cols].astype(F32).T
        dst_scr[g, i, 0:LANES, :] = t.astype(BF16)
        if extra:
            dst_scr[g, i, LANES:LANES + extra, :] = jnp.ones((extra, ATT_TILE), BF16)


def _park_scores(dst_scr, tiles):
    for c, s in enumerate(tiles):
        dst_scr[c] = s


def _causal_blocks(qi, n_tiles, n_chains, acc_scr, park_a, park_b, scores, values_t):
    t = ATT_TILE
    chains = range(n_chains)

    def park(j, dst_scr, diagonal):
        _park_scores(dst_scr, scores(qi, j, diagonal))

    def park_next_tile():
        nxt = jnp.minimum(qi + 1, n_tiles - 1)
        _park_scores(park_a, scores(nxt, nxt, True))

    def consume(j, src_scr, maxes):
        out, pend = [], []
        for c in chains:
            s = src_scr[c]
            m_new = jnp.maximum(maxes[c], jnp.max(s, axis=0, keepdims=True))
            p = jnp.exp2(s - m_new).astype(BF16)
            pend.append((jnp.exp2(maxes[c] - m_new), jnp.dot(values_t(c, j), p, preferred_element_type=F32)))
            out.append(m_new)
        return tuple(out), pend

    def accumulate(*pending):
        for c in chains:
            acc = acc_scr[c]
            for pend in pending:
                a, pv = pend[c]
                acc = acc * a + pv
            acc_scr[c] = acc

    acc_scr[...] = jnp.zeros_like(acc_scr)
    init = (jnp.full((1, t), NEG_INF, F32),) * n_chains

    def pair(pi, stats):
        j = 2 * pi
        park(j, park_b, False)
        stats, pend_a = consume(jnp.where(pi == 0, qi, j - 1), park_a, stats)
        park(j + 1, park_a, False)
        stats, pend_b = consume(j, park_b, stats)
        accumulate(pend_a, pend_b)
        return stats

    n_pairs = qi // 2
    stats = lax.fori_loop(0, n_pairs, pair, init)
    in_a = jnp.where(n_pairs == 0, qi, 2 * n_pairs - 1)

    def odd_tail(stats):
        park(qi - 1, park_b, False)
        stats, pend_a = consume(in_a, park_a, stats)
        stats, pend_b = consume(qi - 1, park_b, stats)
        accumulate(pend_a, pend_b)
        return stats

    def even_tail(stats):
        stats, pend_a = consume(in_a, park_a, stats)
        accumulate(pend_a)
        return stats

    lax.cond(qi % 2 == 1, odd_tail, even_tail, stats)
    park_next_tile()


def _normalised(acc_scr, c):
    return acc_scr[c, 0:LANES, :] / acc_scr[c, LANES:LANES + 1, :]


def _key_block(k_ref, g, j):
    return k_ref[pl.ds(pl.multiple_of(j * ATT_TILE, ATT_TILE), ATT_TILE), g * LANES:(g + 1) * LANES]


def _diff_attn_kernel(lam_ref, q_ref, k_ref, v_ref, gsub_ref, o_ref,
                      q1t_scr, q2t_scr, vt_scr, acc_scr, park_a, park_b, *, out_scale):
    t = ATT_TILE
    n_tiles = q_ref.shape[0] // t
    heads = range(DIFF_HEADS)
    first_half = lax.broadcasted_iota(jnp.int32, (LANES, t), 0) < D_A
    for g in heads:
        for i in range(n_tiles):
            qt = q_ref[i * t:(i + 1) * t, g * LANES:(g + 1) * LANES].astype(F32).T
            q1t_scr[g, i] = jnp.where(first_half, qt, 0.0).astype(BF16)
            q2t_scr[g, i] = jnp.where(first_half, 0.0, qt).astype(BF16)
        _stage_transposed(v_ref, g, vt_scr, n_tiles)
    lam = lam_ref[0, 0]
    causal = (lax.broadcasted_iota(jnp.int32, (t, t), 0) <= lax.broadcasted_iota(jnp.int32, (t, t), 1))

    def scores(q_tile, j, diagonal):
        out = []
        for g in heads:
            kj = _key_block(k_ref, g, j)
            for qt_scr in (q1t_scr, q2t_scr):
                s = jnp.dot(kj, qt_scr[g, q_tile], preferred_element_type=F32)
                out.append(jnp.where(causal, s, NEG_INF) if diagonal else s)
        return out

    _park_scores(park_a, scores(0, 0, True))

    def q_tile(qi, carry):
        _causal_blocks(qi, n_tiles, 2 * DIFF_HEADS, acc_scr, park_a, park_b, scores,
                       lambda c, j: vt_scr[c // 2, j])
        for g in heads:
            ot = _normalised(acc_scr, 2 * g) - lam * _normalised(acc_scr, 2 * g + 1)
            ot = ot * lax.rsqrt(jnp.mean(ot * ot, axis=0, keepdims=True) + EPS)
            o = ot.T * (gsub_ref[...] * out_scale)
            o_ref[pl.ds(pl.multiple_of(qi * t, t), t), g * LANES:(g + 1) * LANES] = o.astype(o_ref.dtype)
        return carry

    lax.fori_loop(0, n_tiles, q_tile, 0)


def _head_group_spec(s, heads):
    return pl.BlockSpec((None, s, heads * LANES), lambda bi, h: (bi, 0, h))


def _diff_attention(lam, q, k, v, g_sub, out_scale):
    b, s, w = q.shape
    n_tiles = s // ATT_TILE
    hg = DIFF_HEADS
    tiles_t = pltpu.VMEM((hg, n_tiles, LANES, ATT_TILE), BF16)
    values_t = pltpu.VMEM((hg, n_tiles, ATT_ACC_ROWS, ATT_TILE), BF16)
    parking = pltpu.VMEM((2 * hg, ATT_TILE, ATT_TILE), F32)
    return pl.pallas_call(
        functools.partial(_diff_attn_kernel, out_scale=out_scale),
        grid=(b, w // (hg * LANES)),
        in_specs=[pl.BlockSpec(memory_space=pltpu.SMEM)] + [_head_group_spec(s, hg)] * 3
        + [pl.BlockSpec((1, LANES), lambda bi, h: (0, 0))],
        out_specs=_head_group_spec(s, hg),
        out_shape=jax.ShapeDtypeStruct((b, s, w), BF16),
        scratch_shapes=[tiles_t, tiles_t, values_t, pltpu.VMEM((2 * hg, ATT_ACC_ROWS, ATT_TILE), F32),
                        parking, parking],
        compiler_params=_cparams("parallel", "parallel"),
        name="diff_attn",
    )(lam, q, k, v, g_sub)


def _split_hi_lo(x):
    hi = x.astype(BF16)
    return hi, (x - hi.astype(F32)).astype(BF16)


def _topk_bias(g, n_valid):
    nb = g.shape[0]
    idx = lax.broadcasted_iota(jnp.int32, g.shape, 0)
    rank = jnp.zeros(g.shape, jnp.int32)
    for m in range(nb):
        gm = g[m:m + 1, :]
        beats = jnp.where(gm > g, 1, jnp.where((gm == g) & (idx > m), 1, 0))
        rank = rank + jnp.where(n_valid > m, beats, 0)
    sel = (idx < n_valid) & (rank < MOBA_TOPK)
    return jnp.where(sel, 0.0, NEG_INF)


def _moba_kernel(q_ref, k_ref, v_ref, o_ref, qt_scr, vt_scr, kmh_scr, kml_scr, bias_scr, acc_scr,
                 park_a, park_b):
    t = ATT_TILE
    n_tiles = q_ref.shape[0] // t
    heads = range(MOBA_HEADS)
    for g in heads:
        _stage_transposed(q_ref, g, qt_scr, n_tiles)
        _stage_transposed(v_ref, g, vt_scr, n_tiles)
        k = k_ref[:, g * LANES:(g + 1) * LANES].astype(F32)
        kmh_scr[g], kml_scr[g] = _split_hi_lo(jnp.mean(k.reshape(n_tiles, t, LANES), axis=1))
    causal = (lax.broadcasted_iota(jnp.int32, (t, t), 0) <= lax.broadcasted_iota(jnp.int32, (t, t), 1))

    def scores(q_tile, j, diagonal):
        out = []
        for g in heads:
            s = jnp.dot(_key_block(k_ref, g, j), qt_scr[g, q_tile], preferred_element_type=F32)
            out.append(jnp.where(causal, s, NEG_INF) if diagonal else s + bias_scr[g, j])
        return out

    def select_blocks(q_tile):
        for g in heads:
            qt = qt_scr[g, q_tile]
            gate = (jnp.dot(kmh_scr[g], qt, preferred_element_type=F32)
                    + jnp.dot(kml_scr[g], qt, preferred_element_type=F32))
            bias = _topk_bias(gate, q_tile)
            for n in range(n_tiles):
                bias_scr[g, n] = bias[n:n + 1, :]

    _park_scores(park_a, scores(0, 0, True))

    def q_tile(qi, carry):
        _causal_blocks(qi, n_tiles, MOBA_HEADS, acc_scr, park_a, park_b, scores, lambda c, j: vt_scr[c, j])
        select_blocks(jnp.minimum(qi + 1, n_tiles - 1))
        for g in heads:
            o = _normalised(acc_scr, g).T
            o_ref[pl.ds(pl.multiple_of(qi * t, t), t), g * LANES:(g + 1) * LANES] = o.astype(o_ref.dtype)
        return carry

    lax.fori_loop(0, n_tiles, q_tile, 0)


def _moba_attention(q, k, v):
    b, s, w = q.shape
    n_tiles = s // ATT_TILE
    hg = MOBA_HEADS
    tiles_t = pltpu.VMEM((hg, n_tiles, LANES, ATT_TILE), BF16)
    values_t = pltpu.VMEM((hg, n_tiles, ATT_ACC_ROWS, ATT_TILE), BF16)
    means = pltpu.VMEM((hg, n_tiles, LANES), BF16)
    parking = pltpu.VMEM((hg, ATT_TILE, ATT_TILE), F32)
    return pl.pallas_call(
        _moba_kernel,
        grid=(b, w // (hg * LANES)),
        in_specs=[_head_group_spec(s, hg)] * 3,
        out_specs=_head_group_spec(s, hg),
        out_shape=jax.ShapeDtypeStruct((b, s, w), BF16),
        scratch_shapes=[tiles_t, values_t, means, means, pltpu.VMEM((hg, n_tiles, 1, ATT_TILE), F32),
                        pltpu.VMEM((hg, ATT_ACC_ROWS, ATT_TILE), F32), parking, parking],
        compiler_params=_cparams("parallel", "parallel"),
        name="moba_attn",
    )(q, k, v)


def _block_diag_queries(q, n_rep):
    t, w = q.shape
    used = n_rep * (w // LANES) * t
    rows = -(-used // LANES) * LANES
    tiled = jnp.concatenate([q] * (rows // t), axis=0)
    r = lax.broadcasted_iota(jnp.int32, (rows, w), 0)
    c = lax.broadcasted_iota(jnp.int32, (rows, w), 1)
    head = (r // t) % (w // LANES)
    part = r // (t * (w // LANES))
    keep = (c // LANES == head) & ((c % LANES) // (LANES // n_rep) == part) & (r < used)
    return jnp.where(keep, tiled, 0.0).astype(BF16)


def _dot_nt(a, b):
    return lax.dot_general(a, b, (((1,), (1,)), ((), ())), preferred_element_type=F32)


def _head_diagonal(x, t):
    return jnp.concatenate(
        [x[h * t:(h + 1) * t, h * LANES:(h + 1) * LANES] for h in range(x.shape[1] // LANES)], axis=1)


def _load_page(ref):
    n_head = H_A
    slots = ref.shape[0] // n_head
    return jnp.concatenate([ref[pl.ds(h, slots, stride=n_head), :] for h in range(n_head)], axis=1)


def _load_block(pages, b, per_blk):
    rows = jnp.concatenate([_load_page(pages[b * per_blk + i]) for i in range(per_blk)], axis=0)
    return rows.astype(BF16), rows


def _pad_rows(x, rows):
    return jnp.concatenate([x, jnp.zeros((rows - x.shape[0], x.shape[1]), x.dtype)], axis=0)


def _topk_bias_lanes(g, n_valid):
    idx = lax.broadcasted_iota(jnp.int32, g.shape, 1)
    rank = jnp.zeros(g.shape, jnp.int32)
    for m in range(n_valid):
        gm = g[:, m:m + 1]
        rank = rank + jnp.where(gm > g, 1, jnp.where((gm == g) & (idx > m), 1, 0))
    sel = (idx < n_valid) & (rank < MOBA_TOPK)
    return jnp.where(sel, 0.0, NEG_INF)


def _paged_attention(qbd, tq, kn_ref, vn_ref, k_pages, v_pages, s_scr, block_bias):
    page = k_pages[0].shape[0] // H_A
    per_blk = MOBA_BLOCK // page
    n_blk = len(k_pages) // per_blk
    blk_max, blk_sum = [], []
    for b in range(n_blk):
        kb16, kb = _load_block(k_pages, b, per_blk)
        if block_bias is not None:
            blk_sum.append(jnp.sum(kb, axis=0, keepdims=True))
        s = _dot_nt(qbd, kb16)
        s_scr[:, b * MOBA_BLOCK:(b + 1) * MOBA_BLOCK] = s
        blk_max.append(jnp.max(s, axis=1, keepdims=True))
    s_new = _dot_nt(qbd, _pad_rows(kn_ref[...].astype(BF16), LANES))
    key_t = lax.broadcasted_iota(jnp.int32, s_new.shape, 1)
    qry_t = lax.broadcasted_iota(jnp.int32, s_new.shape, 0) % tq
    s_new = jnp.where(key_t <= qry_t, s_new, NEG_INF)
    bias = None
    if block_bias is not None:
        bias = block_bias(jnp.concatenate(blk_sum, axis=0) * (1.0 / MOBA_BLOCK))
    m = jnp.max(s_new, axis=1, keepdims=True)
    for b in range(n_blk):
        m = jnp.maximum(m, blk_max[b] if bias is None else blk_max[b] + bias[:, b:b + 1])
    p_new = jnp.exp2(s_new - m)
    l_new = jnp.sum(p_new, axis=1, keepdims=True)
    acc = jnp.dot(p_new.astype(BF16), _pad_rows(vn_ref[...].astype(BF16), LANES), preferred_element_type=F32)
    l_run = jnp.zeros((qbd.shape[0], MOBA_BLOCK), F32)
    for b in range(n_blk):
        s = s_scr[:, b * MOBA_BLOCK:(b + 1) * MOBA_BLOCK]
        if bias is not None:
            s = s + bias[:, b:b + 1]
        p = jnp.exp2(s - m)
        l_run = l_run + p
        acc = acc + jnp.dot(p.astype(BF16), _load_block(v_pages, b, per_blk)[0], preferred_element_type=F32)
    return acc / (l_new + jnp.sum(l_run, axis=1, keepdims=True))


def _sample_diff_kernel(n_pages, out_scale, pt_ref, lam_ref, q_ref, kn_ref, vn_ref, gsub_ref, *rest):
    k_pages = rest[:n_pages]
    v_pages = rest[n_pages:2 * n_pages]
    o_ref, s_scr = rest[2 * n_pages], rest[2 * n_pages + 1]
    tq = q_ref.shape[0]
    qbd = _block_diag_queries(q_ref[...].astype(F32), 2)
    acc = _paged_attention(qbd, tq, kn_ref, vn_ref, k_pages, v_pages, s_scr, None)
    half = (q_ref.shape[1] // LANES) * tq
    o = _head_diagonal(acc[:half] - lam_ref[0, 0] * acc[half:2 * half], tq)
    g = gsub_ref[...]
    for h in range(o.shape[1] // LANES):
        cols = slice(h * LANES, (h + 1) * LANES)
        o_ref[:, cols] = (_rms(o[:, cols], g) * out_scale).astype(o_ref.dtype)


def _sample_moba_kernel(n_pages, pt_ref, q_ref, kn_ref, vn_ref, *rest):
    k_pages = rest[:n_pages]
    v_pages = rest[n_pages:2 * n_pages]
    o_ref, s_scr = rest[2 * n_pages], rest[2 * n_pages + 1]
    tq = q_ref.shape[0]
    qbd = _block_diag_queries(q_ref[...].astype(F32), 1)

    def block_bias(means):
        n_blk = means.shape[0]
        km_hi, km_lo = _split_hi_lo(_pad_rows(means, -(-n_blk // 16) * 16))
        return _topk_bias_lanes(_dot_nt(qbd, km_hi) + _dot_nt(qbd, km_lo), n_blk)

    acc = _paged_attention(qbd, tq, kn_ref, vn_ref, k_pages, v_pages, s_scr, block_bias)
    o_ref[...] = _head_diagonal(acc, tq).astype(o_ref.dtype)


def _sample_attention(kernel_fn, name, page_table, scalars, q, k_new, v_new, extra, cache_k, cache_v, n_col):
    db, tq, w = q.shape
    n_pages = page_table.shape[1]
    page_rows = cache_k.shape[1]
    page = page_rows // (w // LANES)
    tok = lambda: pl.BlockSpec((None, tq, w), lambda b, pt: (b, 0, 0))
    page_spec = lambda p: pl.BlockSpec((None, page_rows, LANES), lambda b, pt: (pt[b * n_pages + p], 0, 0))
    in_specs = ([pl.BlockSpec(memory_space=pltpu.SMEM) for _ in scalars] + [tok(), tok(), tok()]
                + [pl.BlockSpec((1, LANES), lambda b, pt: (0, 0)) for _ in extra]
                + [page_spec(p) for p in range(n_pages)] * 2)
    return pl.pallas_call(
        kernel_fn,
        grid_spec=pltpu.PrefetchScalarGridSpec(
            num_scalar_prefetch=1, grid=(db,), in_specs=in_specs, out_specs=tok(),
            scratch_shapes=[pltpu.VMEM((n_col, n_pages * page), F32)]),
        out_shape=jax.ShapeDtypeStruct((db, tq, w), BF16),
        compiler_params=_cparams("arbitrary"),
        name=name,
    )(page_table.reshape(-1), *scalars, q, k_new, v_new, *extra, *([cache_k] * n_pages), *([cache_v] * n_pages))


def _post_kernel(x_ref, oa_ref, ob_ref, pe_ref, gmix_ref, wg_ref, wa_ref, wb_ref, wo_ref, gmlp_ref,
                 wup_ref, wdn_ref, gple_ref, wpg_ref, wpp_ref, y_ref):
    dot = functools.partial(jnp.dot, preferred_element_type=F32)
    x = x_ref[...]
    d = x.shape[1]
    xn = _rms(x, gmix_ref[...]).astype(BF16)
    merged = (jax.nn.sigmoid(dot(xn, wg_ref[:, :d])) * dot(oa_ref[...], wa_ref[...])
              + jax.nn.sigmoid(dot(xn, wg_ref[:, d:])) * dot(ob_ref[...], wb_ref[...]))
    h = x + dot(merged.astype(BF16), wo_ref[...])
    u = dot(_rms(h, gmlp_ref[...]).astype(BF16), wup_ref[...])
    h = h + dot(jnp.square(jnp.maximum(u, 0.0)).astype(BF16), wdn_ref[...])
    gate = jax.nn.sigmoid(dot(_rms(h, gple_ref[...]).astype(BF16), wpg_ref[...]))
    y_ref[...] = h + gate * dot(pe_ref[...].astype(BF16), wpp_ref[...])


def _post(x, oa, ob, pe, g_mix, w_in16, weights, tm):
    m, d = x.shape
    row = lambda a: pl.BlockSpec((tm, a.shape[1]), lambda i: (i, 0))
    whole = lambda a: pl.BlockSpec(a.shape, lambda i: (0, 0), pipeline_mode=pl.Buffered(1))
    gate_w = pl.BlockSpec((d, 2 * d), lambda i: (0, N_QKV_SEG // 2), pipeline_mode=pl.Buffered(1))
    acts = (x, oa, ob, pe)
    return pl.pallas_call(
        _post_kernel, grid=(m // tm,),
        in_specs=[row(a) for a in acts] + [whole(g_mix), gate_w] + [whole(a) for a in weights],
        out_specs=pl.BlockSpec((tm, d), lambda i: (i, 0)),
        out_shape=jax.ShapeDtypeStruct((m, d), F32),
        compiler_params=_cparams("parallel"),
        name="post",
    )(*acts, g_mix, w_in16, *weights)


def _row_tile(m, cap):
    t = min(m, cap)
    assert m % t == 0, (m, t)
    return t


def kernel(x_prompt, x_sample, p_prompt, p_sample, cache_a_k, cache_a_v, cache_b_k, cache_b_v, page_table,
           g_mix, w_in, qn_a, kn_a, lam_q1, lam_k1, lam_q2, lam_k2, g_sub_a, qn_b, kn_b,
           w_br_a, w_br_b, w_o, g_mlp, w_up, w_down, g_ple, w_ple_gate, w_ple_proj):
    b, s, d = x_prompt.shape
    db, t, _ = x_sample.shape
    depth = g_mix.shape[0]
    n_pool, page = cache_a_k.shape[1], cache_a_k.shape[2]
    n_pages = page_table.shape[1]
    past = n_pages * page
    w_a, w_b = H_A * 2 * D_A, H_B * D_B
    assert s % ATT_TILE == 0 and ATT_TILE == MOBA_BLOCK and w_a == w_b == d and H_A == H_B
    assert past % MOBA_BLOCK == 0 and MOBA_BLOCK % page == 0 and t <= MOBA_BLOCK and past // MOBA_BLOCK >= MOBA_TOPK
    assert w_in.shape[2] == N_SEG * d

    pos_p = jnp.arange(s, dtype=jnp.int32)
    pos_s = past + jnp.arange(t, dtype=jnp.int32)
    tm_p = _row_tile(s, PROJ_ROWS)
    tm_s = _row_tile(db * t, PROJ_ROWS)
    assert tm_s % t == 0
    tab_p = _rope_tables(pos_p, D_A) + _rope_tables(pos_p, D_B)
    tab_s = tuple(jnp.tile(a, (tm_s // t, 1)) for dd in (D_A, D_B) for a in _rope_tables(pos_s, dd))

    hp = x_prompt.reshape(b * s, d)
    hs = x_sample.reshape(db * t, d)
    caches = ([], [], [], [], [], [], [], [])
    tile2 = lambda g: jnp.tile(g, 2)[None, :]
    for l in range(depth):
        lam_init = 0.8 - 0.6 * math.exp(-0.3 * l)
        lam = (jnp.exp(jnp.sum(lam_q1[l] * lam_k1[l])) - jnp.exp(jnp.sum(lam_q2[l] * lam_k2[l])) + lam_init)
        lam = lam.reshape(1, 1).astype(F32)
        out_scale = 1.0 - lam_init
        g_mix_l = g_mix[l][None, :]
        w_in16 = w_in[l].astype(BF16)
        gains = (tile2(qn_a[l]) * (D_A ** -0.5 * LOG2_E), tile2(kn_a[l]),
                 qn_b[l][None, :] * (D_B ** -0.5 * LOG2_E), kn_b[l][None, :])
        post_w = (w_br_a[l].astype(BF16), w_br_b[l].astype(BF16), w_o[l].astype(BF16), g_mlp[l][None, :],
                  w_up[l].astype(BF16), w_down[l].astype(BF16), g_ple[l][None, :],
                  w_ple_gate[l].astype(BF16), w_ple_proj[l].astype(BF16))
        g_sub = g_sub_a[l][None, :]

        qa, ka, va, qb, kb, vb, ka16, va16, kb16, vb16 = _project(hp, g_mix_l, w_in16, gains, tab_p, tm_p)
        r3 = lambda a: a.reshape(b, s, d)
        oa = _diff_attention(lam, r3(qa), r3(ka16), r3(va16), g_sub, out_scale).reshape(b * s, d)
        ob = _moba_attention(r3(qb), r3(kb16), r3(vb16)).reshape(b * s, d)
        hp = _post(hp, oa, ob, p_prompt[l].reshape(b * s, -1), g_mix_l, w_in16, post_w,
                   _row_tile(b * s, POST_ROWS))
        for dst, a in zip(caches[:4], (ka, va, kb, vb)):
            dst.append(a.reshape(b, s, H_A, LANES))

        qa, ka, va, qb, kb, vb = _project(hs, g_mix_l, w_in16, gains, tab_s, tm_s)[:6]
        r3 = lambda a: a.reshape(db, t, d)
        c3 = lambda c: c[l].reshape(n_pool, page * H_A, LANES)
        n_col = -(-2 * H_A * t // LANES) * LANES
        oa = _sample_attention(functools.partial(_sample_diff_kernel, n_pages, out_scale), "sample_diff_attn",
                               page_table, (lam,), r3(qa), r3(ka), r3(va), (g_sub,),
                               c3(cache_a_k), c3(cache_a_v), n_col)
        n_col = -(-H_B * t // LANES) * LANES
        ob = _sample_attention(functools.partial(_sample_moba_kernel, n_pages), "sample_moba_attn",
                               page_table, (), r3(qb), r3(kb), r3(vb), (), c3(cache_b_k), c3(cache_b_v), n_col)
        hs = _post(hs, oa.reshape(db * t, d), ob.reshape(db * t, d), p_sample[l].reshape(db * t, -1),
                   g_mix_l, w_in16, post_w, _row_tile(db * t, POST_ROWS))
        for dst, a in zip(caches[4:], (ka, va, kb, vb)):
            dst.append(a.reshape(db, t, H_A, LANES))

    return (hp.reshape(b, s, d), hs.reshape(db, t, d)) + tuple(jnp.stack(c) for c in caches)
```

```python
import functools
import math

import jax
import jax.numpy as jnp
from jax import lax
from jax.experimental import pallas as pl
from jax.experimental.pallas import tpu as pltpu

F32 = jnp.float32
BF16 = jnp.bfloat16

LANES = 128
H_A = 8
D_A = 64
H_B = 8
D_B = 128
MOBA_BLOCK = 256
MOBA_TOPK = 3
ROPE_THETA = 10000.0
EPS = 1e-6
N_SEG = 8
N_QKV_SEG = 6
NEG_INF = float("-inf")
VMEM_LIMIT = 56 * 1024 * 1024
ATT_TILE = 256
DIFF_HEADS = 4
MOBA_HEADS = 4
ATT_ACC_ROWS = LANES + 16
LOG2_E = math.log2(math.e)
PROJ_ROWS = 512
POST_ROWS = 256


def _cparams(*sem):
    return pltpu.CompilerParams(dimension_semantics=sem, vmem_limit_bytes=VMEM_LIMIT)


def _rms(x, g):
    return x * lax.rsqrt(jnp.mean(x * x, axis=-1, keepdims=True) + EPS) * g


def _norm_rope(zh, g, c, s, group):
    sq = zh * zh
    lane = lax.broadcasted_iota(jnp.int32, zh.shape, 1)
    if group == LANES:
        ss = jnp.sum(sq, axis=-1, keepdims=True)
    else:
        lo = lane < group
        s_lo = jnp.sum(jnp.where(lo, sq, 0.0), axis=-1, keepdims=True)
        s_all = jnp.sum(sq, axis=-1, keepdims=True)
        ss = jnp.where(lo, s_lo, s_all - s_lo)
    y = zh * lax.rsqrt(ss * (1.0 / group) + EPS) * g
    half = group // 2
    if group == LANES:
        sw = pltpu.roll(y, half, axis=1)
    else:
        first = (lane % group) < half
        sw = jnp.where(first, pltpu.roll(y, LANES - half, axis=1), pltpu.roll(y, half, axis=1))
    return y * c + sw * s


def _proj_kernel(x_ref, gmix_ref, w_ref, gqa_ref, gka_ref, gqb_ref, gkb_ref, ca_ref, sa_ref, cb_ref, sb_ref,
                 qa_ref, ka_ref, va_ref, qb_ref, kb_ref, vb_ref, ka16_ref, va16_ref, kb16_ref, vb16_ref):
    xn = _rms(x_ref[...], gmix_ref[...]).astype(BF16)
    width = qa_ref.shape[1]

    def segment(i):
        return jnp.dot(xn, w_ref[:, i * width:(i + 1) * width], preferred_element_type=F32)

    def store_rope(o_refs, z, group, g_ref, c_ref, s_ref):
        g, c, s = g_ref[...], c_ref[...], s_ref[...]
        for h in range(width // LANES):
            cols = slice(h * LANES, (h + 1) * LANES)
            y = _norm_rope(z[:, cols], g, c, s, group)
            for o_ref in o_refs:
                o_ref[:, cols] = y.astype(o_ref.dtype)

    def store_plain(o_refs, z):
        for o_ref in o_refs:
            o_ref[...] = z.astype(o_ref.dtype)

    store_rope((qa_ref,), segment(0), D_A, gqa_ref, ca_ref, sa_ref)
    store_rope((ka_ref, ka16_ref), segment(1), D_A, gka_ref, ca_ref, sa_ref)
    store_plain((va_ref, va16_ref), segment(2))
    store_rope((qb_ref,), segment(3), D_B, gqb_ref, cb_ref, sb_ref)
    store_rope((kb_ref, kb16_ref), segment(4), D_B, gkb_ref, cb_ref, sb_ref)
    store_plain((vb_ref, vb16_ref), segment(5))


def _project(x, g_mix, w_in16, gains, tables, tm):
    m, d = x.shape
    width = w_in16.shape[1] // N_SEG
    n_tab = tables[0].shape[0] // tm
    row = lambda w: pl.BlockSpec((tm, w), lambda i: (i, 0))
    const = lambda a: pl.BlockSpec(a.shape, lambda i: (0, 0), pipeline_mode=pl.Buffered(1))
    w_spec = pl.BlockSpec((d, N_QKV_SEG * width), lambda i: (0, 0), pipeline_mode=pl.Buffered(1))
    tab = pl.BlockSpec((tm, LANES), lambda i: (i % n_tab, 0))
    out_dtypes = (BF16, F32, F32, BF16, F32, F32, BF16, BF16, BF16, BF16)
    return pl.pallas_call(
        _proj_kernel, grid=(m // tm,),
        in_specs=[row(d), const(g_mix), w_spec] + [const(g) for g in gains] + [tab] * 4,
        out_specs=[row(width)] * len(out_dtypes),
        out_shape=[jax.ShapeDtypeStruct((m, width), dt) for dt in out_dtypes],
        compiler_params=_cparams("parallel"),
        name="proj",
    )(x, g_mix, w_in16, *gains, *tables)


def _rope_tables(pos, d):
    half = d // 2
    inv = ROPE_THETA ** (-jnp.arange(half, dtype=F32) * 2.0 / d)
    lane = jnp.arange(LANES)
    ang = pos.astype(F32)[:, None] * inv[lane % half][None, :]
    first_half = ((lane % d) < half)[None, :]
    return jnp.cos(ang), jnp.where(first_half, -jnp.sin(ang), jnp.sin(ang))


def _stage_transposed(src_ref, g, dst_scr, n_tiles):
    cols = slice(g * LANES, (g + 1) * LANES)
    extra = dst_scr.shape[2] - LANES
    for i in range(n_tiles):
        t = src_ref[i * ATT_TILE:(i + 1) * ATT_TILE, cols].astype(F32).T
        dst_scr[g, i, 0:LANES, :] = t.astype(BF16)
        if extra:
            dst_scr[g, i, LANES:LANES + extra, :] = jnp.ones((extra, ATT_TILE), BF16)


def _park_scores(dst_scr, tiles):
    for c, s in enumerate(tiles):
        dst_scr[c] = s


def _causal_blocks(qi, n_tiles, n_chains, acc_scr, park_a, park_b, scores, values_t):
    t = ATT_TILE
    chains = range(n_chains)

    def park(j, dst_scr, diagonal):
        _park_scores(dst_scr, scores(qi, j, diagonal))

    def park_next_tile():
        nxt = jnp.minimum(qi + 1, n_tiles - 1)
        _park_scores(park_a, scores(nxt, nxt, True))

    def consume(j, src_scr, maxes):
        out, pend = [], []
        for c in chains:
            s = src_scr[c]
            m_new = jnp.maximum(maxes[c], jnp.max(s, axis=0, keepdims=True))
            p = jnp.exp2(s - m_new).astype(BF16)
            pend.append((jnp.exp2(maxes[c] - m_new), jnp.dot(values_t(c, j), p, preferred_element_type=F32)))
            out.append(m_new)
        return tuple(out), pend

    def accumulate(*pending):
        for c in chains:
            acc = acc_scr[c]
            for pend in pending:
                a, pv = pend[c]
                acc = acc * a + pv
            acc_scr[c] = acc

    acc_scr[...] = jnp.zeros_like(acc_scr)
    init = (jnp.full((1, t), NEG_INF, F32),) * n_chains

    def pair(pi, stats):
        j = 2 * pi
        park(j, park_b, False)
        stats, pend_a = consume(jnp.where(pi == 0, qi, j - 1), park_a, stats)
        park(j + 1, park_a, False)
        stats, pend_b = consume(j, park_b, stats)
        accumulate(pend_a, pend_b)
        return stats

    n_pairs = qi // 2
    stats = lax.fori_loop(0, n_pairs, pair, init)
    in_a = jnp.where(n_pairs == 0, qi, 2 * n_pairs - 1)

    def odd_tail(stats):
        park(qi - 1, park_b, False)
        stats, pend_a = consume(in_a, park_a, stats)
        stats, pend_b = consume(qi - 1, park_b, stats)
        accumulate(pend_a, pend_b)
        return stats

    def even_tail(stats):
        stats, pend_a = consume(in_a, park_a, stats)
        accumulate(pend_a)
        return stats

    lax.cond(qi % 2 == 1, odd_tail, even_tail, stats)
    park_next_tile()


def _normalised(acc_scr, c):
    return acc_scr[c, 0:LANES, :] / acc_scr[c, LANES:LANES + 1, :]


def _key_block(k_ref, g, j):
    return k_ref[pl.ds(pl.multiple_of(j * ATT_TILE, ATT_TILE), ATT_TILE), g * LANES:(g + 1) * LANES]


def _diff_attn_kernel(lam_ref, q_ref, k_ref, v_ref, gsub_ref, o_ref,
                      q1t_scr, q2t_scr, vt_scr, acc_scr, park_a, park_b, *, out_scale):
    t = ATT_TILE
    n_tiles = q_ref.shape[0] // t
    heads = range(DIFF_HEADS)
    first_half = lax.broadcasted_iota(jnp.int32, (LANES, t), 0) < D_A
    for g in heads:
        for i in range(n_tiles):
            qt = q_ref[i * t:(i + 1) * t, g * LANES:(g + 1) * LANES].astype(F32).T
            q1t_scr[g, i] = jnp.where(first_half, qt, 0.0).astype(BF16)
            q2t_scr[g, i] = jnp.where(first_half, 0.0, qt).astype(BF16)
        _stage_transposed(v_ref, g, vt_scr, n_tiles)
    lam = lam_ref[0, 0]
    causal = (lax.broadcasted_iota(jnp.int32, (t, t), 0) <= lax.broadcasted_iota(jnp.int32, (t, t), 1))

    def scores(q_tile, j, diagonal):
        out = []
        for g in heads:
            kj = _key_block(k_ref, g, j)
            for qt_scr in (q1t_scr, q2t_scr):
                s = jnp.dot(kj, qt_scr[g, q_tile], preferred_element_type=F32)
                out.append(jnp.where(causal, s, NEG_INF) if diagonal else s)
        return out

    _park_scores(park_a, scores(0, 0, True))

    def q_tile(qi, carry):
        _causal_blocks(qi, n_tiles, 2 * DIFF_HEADS, acc_scr, park_a, park_b, scores,
                       lambda c, j: vt_scr[c // 2, j])
        for g in heads:
            ot = _normalised(acc_scr, 2 * g) - lam * _normalised(acc_scr, 2 * g + 1)
            ot = ot * lax.rsqrt(jnp.mean(ot * ot, axis=0, keepdims=True) + EPS)
            o = ot.T * (gsub_ref[...] * out_scale)
            o_ref[pl.ds(pl.multiple_of(qi * t, t), t), g * LANES:(g + 1) * LANES] = o.astype(o_ref.dtype)
        return carry

    lax.fori_loop(0, n_tiles, q_tile, 0)


def _head_group_spec(s, heads):
    return pl.BlockSpec((None, s, heads * LANES), lambda bi, h: (bi, 0, h))


def _diff_attention(lam, q, k, v, g_sub, out_scale):
    b, s, w = q.shape
    n_tiles = s // ATT_TILE
    hg = DIFF_HEADS
    tiles_t = pltpu.VMEM((hg, n_tiles, LANES, ATT_TILE), BF16)
    values_t = pltpu.VMEM((hg, n_tiles, ATT_ACC_ROWS, ATT_TILE), BF16)
    parking = pltpu.VMEM((2 * hg, ATT_TILE, ATT_TILE), F32)
    return pl.pallas_call(
        functools.partial(_diff_attn_kernel, out_scale=out_scale),
        grid=(b, w // (hg * LANES)),
        in_specs=[pl.BlockSpec(memory_space=pltpu.SMEM)] + [_head_group_spec(s, hg)] * 3
        + [pl.BlockSpec((1, LANES), lambda bi, h: (0, 0))],
        out_specs=_head_group_spec(s, hg),
        out_shape=jax.ShapeDtypeStruct((b, s, w), BF16),
        scratch_shapes=[tiles_t, tiles_t, values_t, pltpu.VMEM((2 * hg, ATT_ACC_ROWS, ATT_TILE), F32),
                        parking, parking],
        compiler_params=_cparams("parallel", "parallel"),
        name="diff_attn",
    )(lam, q, k, v, g_sub)


def _split_hi_lo(x):
    hi = x.astype(BF16)
    return hi, (x - hi.astype(F32)).astype(BF16)


def _topk_bias(g, n_valid):
    nb = g.shape[0]
    idx = lax.broadcasted_iota(jnp.int32, g.shape, 0)
    rank = jnp.zeros(g.shape, jnp.int32)
    for m in range(nb):
        gm = g[m:m + 1, :]
        beats = jnp.where(gm > g, 1, jnp.where((gm == g) & (idx > m), 1, 0))
        rank = rank + jnp.where(n_valid > m, beats, 0)
    sel = (idx < n_valid) & (rank < MOBA_TOPK)
    return jnp.where(sel, 0.0, NEG_INF)


def _moba_kernel(q_ref, k_ref, v_ref, o_ref, qt_scr, vt_scr, kmh_scr, kml_scr, bias_scr, acc_scr,
                 park_a, park_b):
    t = ATT_TILE
    n_tiles = q_ref.shape[0] // t
    heads = range(MOBA_HEADS)
    for g in heads:
        _stage_transposed(q_ref, g, qt_scr, n_tiles)
        _stage_transposed(v_ref, g, vt_scr, n_tiles)
        k = k_ref[:, g * LANES:(g + 1) * LANES].astype(F32)
        kmh_scr[g], kml_scr[g] = _split_hi_lo(jnp.mean(k.reshape(n_tiles, t, LANES), axis=1))
    causal = (lax.broadcasted_iota(jnp.int32, (t, t), 0) <= lax.broadcasted_iota(jnp.int32, (t, t), 1))

    def scores(q_tile, j, diagonal):
        out = []
        for g in heads:
            s = jnp.dot(_key_block(k_ref, g, j), qt_scr[g, q_tile], preferred_element_type=F32)
            out.append(jnp.where(causal, s, NEG_INF) if diagonal else s + bias_scr[g, j])
        return out

    def select_blocks(q_tile):
        for g in heads:
            qt = qt_scr[g, q_tile]
            gate = (jnp.dot(kmh_scr[g], qt, preferred_element_type=F32)
                    + jnp.dot(kml_scr[g], qt, preferred_element_type=F32))
            bias = _topk_bias(gate, q_tile)
            for n in range(n_tiles):
                bias_scr[g, n] = bias[n:n + 1, :]

    _park_scores(park_a, scores(0, 0, True))

    def q_tile(qi, carry):
        _causal_blocks(qi, n_tiles, MOBA_HEADS, acc_scr, park_a, park_b, scores, lambda c, j: vt_scr[c, j])
        select_blocks(jnp.minimum(qi + 1, n_tiles - 1))
        for g in heads:
            o = _normalised(acc_scr, g).T
            o_ref[pl.ds(pl.multiple_of(qi * t, t), t), g * LANES:(g + 1) * LANES] = o.astype(o_ref.dtype)
        return carry

    lax.fori_loop(0, n_tiles, q_tile, 0)


def _moba_attention(q, k, v):
    b, s, w = q.shape
    n_tiles = s // ATT_TILE
    hg = MOBA_HEADS
    tiles_t = pltpu.VMEM((hg, n_tiles, LANES, ATT_TILE), BF16)
    values_t = pltpu.VMEM((hg, n_tiles, ATT_ACC_ROWS, ATT_TILE), BF16)
    means = pltpu.VMEM((hg, n_tiles, LANES), BF16)
    parking = pltpu.VMEM((hg, ATT_TILE, ATT_TILE), F32)
    return pl.pallas_call(
        _moba_kernel,
        grid=(b, w // (hg * LANES)),
        in_specs=[_head_group_spec(s, hg)] * 3,
        out_specs=_head_group_spec(s, hg),
        out_shape=jax.ShapeDtypeStruct((b, s, w), BF16),
        scratch_shapes=[tiles_t, values_t, means, means, pltpu.VMEM((hg, n_tiles, 1, ATT_TILE), F32),
                        pltpu.VMEM((hg, ATT_ACC_ROWS, ATT_TILE), F32), parking, parking],
        compiler_params=_cparams("parallel", "parallel"),
        name="moba_attn",
    )(q, k, v)


def _block_diag_queries(q, n_rep):
    t, w = q.shape
    used = n_rep * (w // LANES) * t
    rows = -(-used // LANES) * LANES
    tiled = jnp.concatenate([q] * (rows // t), axis=0)
    r = lax.broadcasted_iota(jnp.int32, (rows, w), 0)
    c = lax.broadcasted_iota(jnp.int32, (rows, w), 1)
    head = (r // t) % (w // LANES)
    part = r // (t * (w // LANES))
    keep = (c // LANES == head) & ((c % LANES) // (LANES // n_rep) == part) & (r < used)
    return jnp.where(keep, tiled, 0.0).astype(BF16)


def _dot_nt(a, b):
    return lax.dot_general(a, b, (((1,), (1,)), ((), ())), preferred_element_type=F32)


def _head_diagonal(x, t):
    return jnp.concatenate(
        [x[h * t:(h + 1) * t, h * LANES:(h + 1) * LANES] for h in range(x.shape[1] // LANES)], axis=1)


def _sequence_copies(pt_ref, seq, n_pages, cache_hbm, buf, slot, sem):
    slots, n_head = cache_hbm.shape[1], cache_hbm.shape[2]
    copies = []
    for p in range(n_pages):
        page = pt_ref[seq * n_pages + p]
        for h in range(n_head):
            copies.append(pltpu.make_async_copy(
                cache_hbm.at[page, :, h, :], buf.at[slot, h, pl.ds(p * slots, slots), :], sem.at[slot]))
    return copies


def _stream_sequence(pt_ref, n_pages, caches_hbm, bufs, sems):
    b, n_seq = pl.program_id(0), pl.num_programs(0)
    slot = b % 2

    def each(seq, s, fn):
        for i, (cache, buf, sem) in enumerate(zip(caches_hbm, bufs, sems)):
            for j, c in enumerate(_sequence_copies(pt_ref, seq, n_pages, cache, buf, s, sem)):
                fn(c, (i + j) % 2)

    @pl.when(b == 0)
    def _():
        each(b, slot, lambda c, prio: c.start(priority=prio))

    @pl.when(b + 1 < n_seq)
    def _():
        each(b + 1, 1 - slot, lambda c, prio: c.start(priority=prio))

    each(b, slot, lambda c, prio: c.wait())
    return slot


def _load_block(buf, slot, b):
    rows = jnp.concatenate([buf[slot, h, b * MOBA_BLOCK:(b + 1) * MOBA_BLOCK, :] for h in range(buf.shape[1])],
                           axis=1)
    return rows.astype(BF16), rows


def _pad_rows(x, rows):
    return jnp.concatenate([x, jnp.zeros((rows - x.shape[0], x.shape[1]), x.dtype)], axis=0)


def _topk_bias_lanes(g, n_valid):
    idx = lax.broadcasted_iota(jnp.int32, g.shape, 1)
    rank = jnp.zeros(g.shape, jnp.int32)
    for m in range(n_valid):
        gm = g[:, m:m + 1]
        rank = rank + jnp.where(gm > g, 1, jnp.where((gm == g) & (idx > m), 1, 0))
    sel = (idx < n_valid) & (rank < MOBA_TOPK)
    return jnp.where(sel, 0.0, NEG_INF)


def _paged_attention(qbd, tq, kn_ref, vn_ref, kbuf, vbuf, slot, s_scr, block_bias):
    n_blk = kbuf.shape[2] // MOBA_BLOCK
    blk_max, blk_sum = [], []
    for b in range(n_blk):
        kb16, kb = _load_block(kbuf, slot, b)
        if block_bias is not None:
            blk_sum.append(jnp.sum(kb, axis=0, keepdims=True))
        s = _dot_nt(qbd, kb16)
        s_scr[:, b * MOBA_BLOCK:(b + 1) * MOBA_BLOCK] = s
        blk_max.append(jnp.max(s, axis=1, keepdims=True))
    s_new = _dot_nt(qbd, _pad_rows(kn_ref[...].astype(BF16), LANES))
    key_t = lax.broadcasted_iota(jnp.int32, s_new.shape, 1)
    qry_t = lax.broadcasted_iota(jnp.int32, s_new.shape, 0) % tq
    s_new = jnp.where(key_t <= qry_t, s_new, NEG_INF)
    bias = None
    if block_bias is not None:
        bias = block_bias(jnp.concatenate(blk_sum, axis=0) * (1.0 / MOBA_BLOCK))
    m = jnp.max(s_new, axis=1, keepdims=True)
    for b in range(n_blk):
        m = jnp.maximum(m, blk_max[b] if bias is None else blk_max[b] + bias[:, b:b + 1])
    p_new = jnp.exp2(s_new - m)
    l_new = jnp.sum(p_new, axis=1, keepdims=True)
    acc = jnp.dot(p_new.astype(BF16), _pad_rows(vn_ref[...].astype(BF16), LANES), preferred_element_type=F32)
    l_run = jnp.zeros((qbd.shape[0], MOBA_BLOCK), F32)
    for b in range(n_blk):
        s = s_scr[:, b * MOBA_BLOCK:(b + 1) * MOBA_BLOCK]
        if bias is not None:
            s = s + bias[:, b:b + 1]
        p = jnp.exp2(s - m)
        l_run = l_run + p
        acc = acc + jnp.dot(p.astype(BF16), _load_block(vbuf, slot, b)[0], preferred_element_type=F32)
    return acc / (l_new + jnp.sum(l_run, axis=1, keepdims=True))


def _sample_diff_kernel(n_pages, out_scale, pt_ref, lam_ref, q_ref, kn_ref, vn_ref, gsub_ref, ck_hbm, cv_hbm,
                        o_ref, kbuf, vbuf, ksem, vsem, s_scr):
    slot = _stream_sequence(pt_ref, n_pages, (ck_hbm, cv_hbm), (kbuf, vbuf), (ksem, vsem))
    tq = q_ref.shape[0]
    qbd = _block_diag_queries(q_ref[...].astype(F32), 2)
    acc = _paged_attention(qbd, tq, kn_ref, vn_ref, kbuf, vbuf, slot, s_scr, None)
    half = (q_ref.shape[1] // LANES) * tq
    o = _head_diagonal(acc[:half] - lam_ref[0, 0] * acc[half:2 * half], tq)
    g = gsub_ref[...]
    for h in range(o.shape[1] // LANES):
        cols = slice(h * LANES, (h + 1) * LANES)
        o_ref[:, cols] = (_rms(o[:, cols], g) * out_scale).astype(o_ref.dtype)


def _sample_moba_kernel(n_pages, pt_ref, q_ref, kn_ref, vn_ref, ck_hbm, cv_hbm,
                        o_ref, kbuf, vbuf, ksem, vsem, s_scr):
    slot = _stream_sequence(pt_ref, n_pages, (ck_hbm, cv_hbm), (kbuf, vbuf), (ksem, vsem))
    tq = q_ref.shape[0]
    qbd = _block_diag_queries(q_ref[...].astype(F32), 1)

    def block_bias(means):
        n_blk = means.shape[0]
        km_hi, km_lo = _split_hi_lo(_pad_rows(means, -(-n_blk // 16) * 16))
        return _topk_bias_lanes(_dot_nt(qbd, km_hi) + _dot_nt(qbd, km_lo), n_blk)

    acc = _paged_attention(qbd, tq, kn_ref, vn_ref, kbuf, vbuf, slot, s_scr, block_bias)
    o_ref[...] = _head_diagonal(acc, tq).astype(o_ref.dtype)


def _sample_attention(kernel_fn, name, page_table, scalars, q, k_new, v_new, extra, cache_k, cache_v, n_col):
    db, tq, w = q.shape
    n_pages = page_table.shape[1]
    n_head = w // LANES
    past = n_pages * cache_k.shape[1]
    tok = lambda: pl.BlockSpec((None, tq, w), lambda b, pt: (b, 0, 0))
    in_specs = ([pl.BlockSpec(memory_space=pltpu.SMEM) for _ in scalars] + [tok(), tok(), tok()]
                + [pl.BlockSpec((1, LANES), lambda b, pt: (0, 0)) for _ in extra]
                + [pl.BlockSpec(memory_space=pl.ANY)] * 2)
    seq_buf = pltpu.VMEM((2, n_head, past, LANES), F32)
    return pl.pallas_call(
        kernel_fn,
        grid_spec=pltpu.PrefetchScalarGridSpec(
            num_scalar_prefetch=1, grid=(db,), in_specs=in_specs, out_specs=tok(),
            scratch_shapes=[seq_buf, seq_buf, pltpu.SemaphoreType.DMA((2,)), pltpu.SemaphoreType.DMA((2,)),
                            pltpu.VMEM((n_col, past), F32)]),
        out_shape=jax.ShapeDtypeStruct((db, tq, w), BF16),
        compiler_params=_cparams("arbitrary"),
        name=name,
    )(page_table.reshape(-1), *scalars, q, k_new, v_new, *extra, cache_k, cache_v)


def _post_kernel(x_ref, oa_ref, ob_ref, pe_ref, gmix_ref, wg_ref, wa_ref, wb_ref, wo_ref, gmlp_ref,
                 wup_ref, wdn_ref, gple_ref, wpg_ref, wpp_ref, y_ref):
    dot = functools.partial(jnp.dot, preferred_element_type=F32)
    x = x_ref[...]
    d = x.shape[1]
    xn = _rms(x, gmix_ref[...]).astype(BF16)
    merged = (jax.nn.sigmoid(dot(xn, wg_ref[:, :d])) * dot(oa_ref[...], wa_ref[...])
              + jax.nn.sigmoid(dot(xn, wg_ref[:, d:])) * dot(ob_ref[...], wb_ref[...]))
    h = x + dot(merged.astype(BF16), wo_ref[...])
    u = dot(_rms(h, gmlp_ref[...]).astype(BF16), wup_ref[...])
    h = h + dot(jnp.square(jnp.maximum(u, 0.0)).astype(BF16), wdn_ref[...])
    gate = jax.nn.sigmoid(dot(_rms(h, gple_ref[...]).astype(BF16), wpg_ref[...]))
    y_ref[...] = h + gate * dot(pe_ref[...].astype(BF16), wpp_ref[...])


def _post(x, oa, ob, pe, g_mix, w_in16, weights, tm):
    m, d = x.shape
    row = lambda a: pl.BlockSpec((tm, a.shape[1]), lambda i: (i, 0))
    whole = lambda a: pl.BlockSpec(a.shape, lambda i: (0, 0), pipeline_mode=pl.Buffered(1))
    gate_w = pl.BlockSpec((d, 2 * d), lambda i: (0, N_QKV_SEG // 2), pipeline_mode=pl.Buffered(1))
    acts = (x, oa, ob, pe)
    return pl.pallas_call(
        _post_kernel, grid=(m // tm,),
        in_specs=[row(a) for a in acts] + [whole(g_mix), gate_w] + [whole(a) for a in weights],
        out_specs=pl.BlockSpec((tm, d), lambda i: (i, 0)),
        out_shape=jax.ShapeDtypeStruct((m, d), F32),
        compiler_params=_cparams("parallel"),
        name="post",
    )(*acts, g_mix, w_in16, *weights)


def _row_tile(m, cap):
    t = min(m, cap)
    assert m % t == 0, (m, t)
    return t


def kernel(x_prompt, x_sample, p_prompt, p_sample, cache_a_k, cache_a_v, cache_b_k, cache_b_v, page_table,
           g_mix, w_in, qn_a, kn_a, lam_q1, lam_k1, lam_q2, lam_k2, g_sub_a, qn_b, kn_b,
           w_br_a, w_br_b, w_o, g_mlp, w_up, w_down, g_ple, w_ple_gate, w_ple_proj):
    b, s, d = x_prompt.shape
    db, t, _ = x_sample.shape
    depth = g_mix.shape[0]
    n_pool, page = cache_a_k.shape[1], cache_a_k.shape[2]
    n_pages = page_table.shape[1]
    past = n_pages * page
    w_a, w_b = H_A * 2 * D_A, H_B * D_B
    assert s % ATT_TILE == 0 and ATT_TILE == MOBA_BLOCK and w_a == w_b == d and H_A == H_B
    assert past % MOBA_BLOCK == 0 and MOBA_BLOCK % page == 0 and t <= MOBA_BLOCK and past // MOBA_BLOCK >= MOBA_TOPK
    assert w_in.shape[2] == N_SEG * d

    pos_p = jnp.arange(s, dtype=jnp.int32)
    pos_s = past + jnp.arange(t, dtype=jnp.int32)
    tm_p = _row_tile(s, PROJ_ROWS)
    tm_s = _row_tile(db * t, PROJ_ROWS)
    assert tm_s % t == 0
    tab_p = _rope_tables(pos_p, D_A) + _rope_tables(pos_p, D_B)
    tab_s = tuple(jnp.tile(a, (tm_s // t, 1)) for dd in (D_A, D_B) for a in _rope_tables(pos_s, dd))

    hp = x_prompt.reshape(b * s, d)
    hs = x_sample.reshape(db * t, d)
    caches = ([], [], [], [], [], [], [], [])
    tile2 = lambda g: jnp.tile(g, 2)[None, :]
    for l in range(depth):
        lam_init = 0.8 - 0.6 * math.exp(-0.3 * l)
        lam = (jnp.exp(jnp.sum(lam_q1[l] * lam_k1[l])) - jnp.exp(jnp.sum(lam_q2[l] * lam_k2[l])) + lam_init)
        lam = lam.reshape(1, 1).astype(F32)
        out_scale = 1.0 - lam_init
        g_mix_l = g_mix[l][None, :]
        w_in16 = w_in[l].astype(BF16)
        gains = (tile2(qn_a[l]) * (D_A ** -0.5 * LOG2_E), tile2(kn_a[l]),
                 qn_b[l][None, :] * (D_B ** -0.5 * LOG2_E), kn_b[l][None, :])
        post_w = (w_br_a[l].astype(BF16), w_br_b[l].astype(BF16), w_o[l].astype(BF16), g_mlp[l][None, :],
                  w_up[l].astype(BF16), w_down[l].astype(BF16), g_ple[l][None, :],
                  w_ple_gate[l].astype(BF16), w_ple_proj[l].astype(BF16))
        g_sub = g_sub_a[l][None, :]

        qa, ka, va, qb, kb, vb, ka16, va16, kb16, vb16 = _project(hp, g_mix_l, w_in16, gains, tab_p, tm_p)
        r3 = lambda a: a.reshape(b, s, d)
        oa = _diff_attention(lam, r3(qa), r3(ka16), r3(va16), g_sub, out_scale).reshape(b * s, d)
        ob = _moba_attention(r3(qb), r3(kb16), r3(vb16)).reshape(b * s, d)
        hp = _post(hp, oa, ob, p_prompt[l].reshape(b * s, -1), g_mix_l, w_in16, post_w,
                   _row_tile(b * s, POST_ROWS))
        for dst, a in zip(caches[:4], (ka, va, kb, vb)):
            dst.append(a.reshape(b, s, H_A, LANES))

        qa, ka, va, qb, kb, vb = _project(hs, g_mix_l, w_in16, gains, tab_s, tm_s)[:6]
        r3 = lambda a: a.reshape(db, t, d)
        c3 = lambda c: c[l]
        n_col = -(-2 * H_A * t // LANES) * LANES
        oa = _sample_attention(functools.partial(_sample_diff_kernel, n_pages, out_scale), "sample_diff_attn",
                               page_table, (lam,), r3(qa), r3(ka), r3(va), (g_sub,),
                               c3(cache_a_k), c3(cache_a_v), n_col)
        n_col = -(-H_B * t // LANES) * LANES
        ob = _sample_attention(functools.partial(_sample_moba_kernel, n_pages), "sample_moba_attn",
                               page_table, (), r3(qb), r3(kb), r3(vb), (), c3(cache_b_k), c3(cache_b_v), n_col)
        hs = _post(hs, oa.reshape(db * t, d), ob.reshape(db * t, d), p_sample[l].reshape(db * t, -1),
                   g_mix_l, w_in16, post_w, _row_tile(db * t, POST_ROWS))
        for dst, a in zip(caches[4:], (ka, va, kb, vb)):
            dst.append(a.reshape(db, t, H_A, LANES))

    return (hp.reshape(b, s, d), hs.reshape(db, t, d)) + tuple(jnp.stack(c) for c in caches)
```

```python
import functools
import math

import jax
import jax.numpy as jnp
from jax import lax
from jax.experimental import pallas as pl
from jax.experimental.pallas import tpu as pltpu

F32 = jnp.float32
BF16 = jnp.bfloat16

LANES = 128
H_A = 8
D_A = 64
H_B = 8
D_B = 128
MOBA_BLOCK = 256
MOBA_TOPK = 3
ROPE_THETA = 10000.0
EPS = 1e-6
N_SEG = 8
N_QKV_SEG = 6
NEG_INF = float("-inf")
VMEM_LIMIT = 56 * 1024 * 1024
ATT_TILE = 256
DIFF_HEADS = 4
MOBA_HEADS = 4
ATT_ACC_ROWS = LANES + 16
LOG2_E = math.log2(math.e)
PROJ_ROWS = 512
POST_ROWS = 512


def _cparams(*sem):
    return pltpu.CompilerParams(dimension_semantics=sem, vmem_limit_bytes=VMEM_LIMIT)


def _rms(x, g):
    return x * lax.rsqrt(jnp.mean(x * x, axis=-1, keepdims=True) + EPS) * g


def _norm_rope(zh, g, c, s, group):
    sq = zh * zh
    lane = lax.broadcasted_iota(jnp.int32, zh.shape, 1)
    if group == LANES:
        ss = jnp.sum(sq, axis=-1, keepdims=True)
    else:
        lo = lane < group
        s_lo = jnp.sum(jnp.where(lo, sq, 0.0), axis=-1, keepdims=True)
        s_all = jnp.sum(sq, axis=-1, keepdims=True)
        ss = jnp.where(lo, s_lo, s_all - s_lo)
    y = zh * lax.rsqrt(ss * (1.0 / group) + EPS) * g
    half = group // 2
    if group == LANES:
        sw = pltpu.roll(y, half, axis=1)
    else:
        first = (lane % group) < half
        sw = jnp.where(first, pltpu.roll(y, LANES - half, axis=1), pltpu.roll(y, half, axis=1))
    return y * c + sw * s


def _proj_kernel(x_ref, gmix_ref, w_ref, gqa_ref, gka_ref, gqb_ref, gkb_ref, ca_ref, sa_ref, cb_ref, sb_ref,
                 qa_ref, ka_ref, va_ref, qb_ref, kb_ref, vb_ref, ka16_ref, va16_ref, kb16_ref, vb16_ref):
    xn = _rms(x_ref[...], gmix_ref[...]).astype(BF16)
    width = qa_ref.shape[1]

    def segment(i):
        return jnp.dot(xn, w_ref[:, i * width:(i + 1) * width], preferred_element_type=F32)

    def store_rope(o_refs, z, group, g_ref, c_ref, s_ref):
        g, c, s = g_ref[...], c_ref[...], s_ref[...]
        for h in range(width // LANES):
            cols = slice(h * LANES, (h + 1) * LANES)
            y = _norm_rope(z[:, cols], g, c, s, group)
            for o_ref in o_refs:
                o_ref[:, cols] = y.astype(o_ref.dtype)

    def store_plain(o_refs, z):
        for o_ref in o_refs:
            o_ref[...] = z.astype(o_ref.dtype)

    store_rope((qa_ref,), segment(0), D_A, gqa_ref, ca_ref, sa_ref)
    store_rope((ka_ref, ka16_ref), segment(1), D_A, gka_ref, ca_ref, sa_ref)
    store_plain((va_ref, va16_ref), segment(2))
    store_rope((qb_ref,), segment(3), D_B, gqb_ref, cb_ref, sb_ref)
    store_rope((kb_ref, kb16_ref), segment(4), D_B, gkb_ref, cb_ref, sb_ref)
    store_plain((vb_ref, vb16_ref), segment(5))


def _project(x, g_mix, w_in16, gains, tables, tm):
    m, d = x.shape
    width = w_in16.shape[1] // N_SEG
    n_tab = tables[0].shape[0] // tm
    row = lambda w: pl.BlockSpec((tm, w), lambda i: (i, 0))
    const = lambda a: pl.BlockSpec(a.shape, lambda i: (0, 0), pipeline_mode=pl.Buffered(1))
    w_spec = pl.BlockSpec((d, N_QKV_SEG * width), lambda i: (0, 0), pipeline_mode=pl.Buffered(1))
    tab = pl.BlockSpec((tm, LANES), lambda i: (i % n_tab, 0))
    out_dtypes = (BF16, F32, F32, BF16, F32, F32, BF16, BF16, BF16, BF16)
    return pl.pallas_call(
        _proj_kernel, grid=(m // tm,),
        in_specs=[row(d), const(g_mix), w_spec] + [const(g) for g in gains] + [tab] * 4,
        out_specs=[row(width)] * len(out_dtypes),
        out_shape=[jax.ShapeDtypeStruct((m, width), dt) for dt in out_dtypes],
        compiler_params=_cparams("parallel"),
        name="proj",
    )(x, g_mix, w_in16, *gains, *tables)


def _rope_tables(pos, d):
    half = d // 2
    inv = ROPE_THETA ** (-jnp.arange(half, dtype=F32) * 2.0 / d)
    lane = jnp.arange(LANES)
    ang = pos.astype(F32)[:, None] * inv[lane % half][None, :]
    first_half = ((lane % d) < half)[None, :]
    return jnp.cos(ang), jnp.where(first_half, -jnp.sin(ang), jnp.sin(ang))


def _stage_transposed(src_ref, g, dst_scr, n_tiles):
    cols = slice(g * LANES, (g + 1) * LANES)
    extra = dst_scr.shape[2] - LANES
    for i in range(n_tiles):
        t = src_ref[i * ATT_TILE:(i + 1) * ATT_TILE, cols].astype(F32).T
        dst_scr[g, i, 0:LANES, :] = t.astype(BF16)
        if extra:
            dst_scr[g, i, LANES:LANES + extra, :] = jnp.ones((extra, ATT_TILE), BF16)


def _park_scores(dst_scr, tiles):
    for c, s in enumerate(tiles):
        dst_scr[c] = s


def _causal_blocks(qi, n_tiles, n_chains, acc_scr, park_a, park_b, scores, values_t):
    t = ATT_TILE
    chains = range(n_chains)

    def park(j, dst_scr, diagonal):
        _park_scores(dst_scr, scores(qi, j, diagonal))

    def park_next_tile():
        nxt = jnp.minimum(qi + 1, n_tiles - 1)
        _park_scores(park_a, scores(nxt, nxt, True))

    def consume(j, src_scr, maxes):
        out, pend = [], []
        for c in chains:
            s = src_scr[c]
            m_new = jnp.maximum(maxes[c], jnp.max(s, axis=0, keepdims=True))
            p = jnp.exp2(s - m_new).astype(BF16)
            pend.append((jnp.exp2(maxes[c] - m_new), jnp.dot(values_t(c, j), p, preferred_element_type=F32)))
            out.append(m_new)
        return tuple(out), pend

    def accumulate(*pending):
        for c in chains:
            acc = acc_scr[c]
            for pend in pending:
                a, pv = pend[c]
                acc = acc * a + pv
            acc_scr[c] = acc

    acc_scr[...] = jnp.zeros_like(acc_scr)
    init = (jnp.full((1, t), NEG_INF, F32),) * n_chains

    def pair(pi, stats):
        j = 2 * pi
        park(j, park_b, False)
        stats, pend_a = consume(jnp.where(pi == 0, qi, j - 1), park_a, stats)
        park(j + 1, park_a, False)
        stats, pend_b = consume(j, park_b, stats)
        accumulate(pend_a, pend_b)
        return stats

    n_pairs = qi // 2
    stats = lax.fori_loop(0, n_pairs, pair, init)
    in_a = jnp.where(n_pairs == 0, qi, 2 * n_pairs - 1)

    def odd_tail(stats):
        park(qi - 1, park_b, False)
        stats, pend_a = consume(in_a, park_a, stats)
        stats, pend_b = consume(qi - 1, park_b, stats)
        accumulate(pend_a, pend_b)
        return stats

    def even_tail(stats):
        stats, pend_a = consume(in_a, park_a, stats)
        accumulate(pend_a)
        return stats

    lax.cond(qi % 2 == 1, odd_tail, even_tail, stats)
    park_next_tile()


def _normalised(acc_scr, c):
    return acc_scr[c, 0:LANES, :] / acc_scr[c, LANES:LANES + 1, :]


def _key_block(k_ref, g, j):
    return k_ref[pl.ds(pl.multiple_of(j * ATT_TILE, ATT_TILE), ATT_TILE), g * LANES:(g + 1) * LANES]


def _diff_attn_kernel(lam_ref, q_ref, k_ref, v_ref, gsub_ref, o_ref,
                      q1t_scr, q2t_scr, vt_scr, acc_scr, park_a, park_b, *, out_scale):
    t = ATT_TILE
    n_tiles = q_ref.shape[0] // t
    heads = range(DIFF_HEADS)
    first_half = lax.broadcasted_iota(jnp.int32, (LANES, t), 0) < D_A
    for g in heads:
        for i in range(n_tiles):
            qt = q_ref[i * t:(i + 1) * t, g * LANES:(g + 1) * LANES].astype(F32).T
            q1t_scr[g, i] = jnp.where(first_half, qt, 0.0).astype(BF16)
            q2t_scr[g, i] = jnp.where(first_half, 0.0, qt).astype(BF16)
        _stage_transposed(v_ref, g, vt_scr, n_tiles)
    lam = lam_ref[0, 0]
    causal = (lax.broadcasted_iota(jnp.int32, (t, t), 0) <= lax.broadcasted_iota(jnp.int32, (t, t), 1))

    def scores(q_tile, j, diagonal):
        out = []
        for g in heads:
            kj = _key_block(k_ref, g, j)
            for qt_scr in (q1t_scr, q2t_scr):
                s = jnp.dot(kj, qt_scr[g, q_tile], preferred_element_type=F32)
                out.append(jnp.where(causal, s, NEG_INF) if diagonal else s)
        return out

    _park_scores(park_a, scores(0, 0, True))

    def q_tile(qi, carry):
        _causal_blocks(qi, n_tiles, 2 * DIFF_HEADS, acc_scr, park_a, park_b, scores,
                       lambda c, j: vt_scr[c // 2, j])
        for g in heads:
            ot = _normalised(acc_scr, 2 * g) - lam * _normalised(acc_scr, 2 * g + 1)
            ot = ot * lax.rsqrt(jnp.mean(ot * ot, axis=0, keepdims=True) + EPS)
            o = ot.T * (gsub_ref[...] * out_scale)
            o_ref[pl.ds(pl.multiple_of(qi * t, t), t), g * LANES:(g + 1) * LANES] = o.astype(o_ref.dtype)
        return carry

    lax.fori_loop(0, n_tiles, q_tile, 0)


def _head_group_spec(s, heads):
    return pl.BlockSpec((None, s, heads * LANES), lambda bi, h: (bi, 0, h))


def _diff_attention(lam, q, k, v, g_sub, out_scale):
    b, s, w = q.shape
    n_tiles = s // ATT_TILE
    hg = DIFF_HEADS
    tiles_t = pltpu.VMEM((hg, n_tiles, LANES, ATT_TILE), BF16)
    values_t = pltpu.VMEM((hg, n_tiles, ATT_ACC_ROWS, ATT_TILE), BF16)
    parking = pltpu.VMEM((2 * hg, ATT_TILE, ATT_TILE), F32)
    return pl.pallas_call(
        functools.partial(_diff_attn_kernel, out_scale=out_scale),
        grid=(b, w // (hg * LANES)),
        in_specs=[pl.BlockSpec(memory_space=pltpu.SMEM)] + [_head_group_spec(s, hg)] * 3
        + [pl.BlockSpec((1, LANES), lambda bi, h: (0, 0))],
        out_specs=_head_group_spec(s, hg),
        out_shape=jax.ShapeDtypeStruct((b, s, w), BF16),
        scratch_shapes=[tiles_t, tiles_t, values_t, pltpu.VMEM((2 * hg, ATT_ACC_ROWS, ATT_TILE), F32),
                        parking, parking],
        compiler_params=_cparams("parallel", "parallel"),
        name="diff_attn",
    )(lam, q, k, v, g_sub)


def _split_hi_lo(x):
    hi = x.astype(BF16)
    return hi, (x - hi.astype(F32)).astype(BF16)


def _topk_bias(g, n_valid):
    nb = g.shape[0]
    idx = lax.broadcasted_iota(jnp.int32, g.shape, 0)
    rank = jnp.zeros(g.shape, jnp.int32)
    for m in range(nb):
        gm = g[m:m + 1, :]
        beats = jnp.where(gm > g, 1, jnp.where((gm == g) & (idx > m), 1, 0))
        rank = rank + jnp.where(n_valid > m, beats, 0)
    sel = (idx < n_valid) & (rank < MOBA_TOPK)
    return jnp.where(sel, 0.0, NEG_INF)


def _moba_kernel(q_ref, k_ref, v_ref, o_ref, qt_scr, vt_scr, kmh_scr, kml_scr, bias_scr, acc_scr,
                 park_a, park_b):
    t = ATT_TILE
    n_tiles = q_ref.shape[0] // t
    heads = range(MOBA_HEADS)
    for g in heads:
        _stage_transposed(q_ref, g, qt_scr, n_tiles)
        _stage_transposed(v_ref, g, vt_scr, n_tiles)
        k = k_ref[:, g * LANES:(g + 1) * LANES].astype(F32)
        kmh_scr[g], kml_scr[g] = _split_hi_lo(jnp.mean(k.reshape(n_tiles, t, LANES), axis=1))
    causal = (lax.broadcasted_iota(jnp.int32, (t, t), 0) <= lax.broadcasted_iota(jnp.int32, (t, t), 1))

    def scores(q_tile, j, diagonal):
        out = []
        for g in heads:
            s = jnp.dot(_key_block(k_ref, g, j), qt_scr[g, q_tile], preferred_element_type=F32)
            out.append(jnp.where(causal, s, NEG_INF) if diagonal else s + bias_scr[g, j])
        return out

    def select_blocks(q_tile):
        for g in heads:
            qt = qt_scr[g, q_tile]
            gate = (jnp.dot(kmh_scr[g], qt, preferred_element_type=F32)
                    + jnp.dot(kml_scr[g], qt, preferred_element_type=F32))
            bias = _topk_bias(gate, q_tile)
            for n in range(n_tiles):
                bias_scr[g, n] = bias[n:n + 1, :]

    _park_scores(park_a, scores(0, 0, True))

    def q_tile(qi, carry):
        _causal_blocks(qi, n_tiles, MOBA_HEADS, acc_scr, park_a, park_b, scores, lambda c, j: vt_scr[c, j])
        select_blocks(jnp.minimum(qi + 1, n_tiles - 1))
        for g in heads:
            o = _normalised(acc_scr, g).T
            o_ref[pl.ds(pl.multiple_of(qi * t, t), t), g * LANES:(g + 1) * LANES] = o.astype(o_ref.dtype)
        return carry

    lax.fori_loop(0, n_tiles, q_tile, 0)


def _moba_attention(q, k, v):
    b, s, w = q.shape
    n_tiles = s // ATT_TILE
    hg = MOBA_HEADS
    tiles_t = pltpu.VMEM((hg, n_tiles, LANES, ATT_TILE), BF16)
    values_t = pltpu.VMEM((hg, n_tiles, ATT_ACC_ROWS, ATT_TILE), BF16)
    means = pltpu.VMEM((hg, n_tiles, LANES), BF16)
    parking = pltpu.VMEM((hg, ATT_TILE, ATT_TILE), F32)
    return pl.pallas_call(
        _moba_kernel,
        grid=(b, w // (hg * LANES)),
        in_specs=[_head_group_spec(s, hg)] * 3,
        out_specs=_head_group_spec(s, hg),
        out_shape=jax.ShapeDtypeStruct((b, s, w), BF16),
        scratch_shapes=[tiles_t, values_t, means, means, pltpu.VMEM((hg, n_tiles, 1, ATT_TILE), F32),
                        pltpu.VMEM((hg, ATT_ACC_ROWS, ATT_TILE), F32), parking, parking],
        compiler_params=_cparams("parallel", "parallel"),
        name="moba_attn",
    )(q, k, v)


def _block_diag_queries(q, n_rep):
    t, w = q.shape
    used = n_rep * (w // LANES) * t
    rows = -(-used // LANES) * LANES
    tiled = jnp.concatenate([q] * (rows // t), axis=0)
    r = lax.broadcasted_iota(jnp.int32, (rows, w), 0)
    c = lax.broadcasted_iota(jnp.int32, (rows, w), 1)
    head = (r // t) % (w // LANES)
    part = r // (t * (w // LANES))
    keep = (c // LANES == head) & ((c % LANES) // (LANES // n_rep) == part) & (r < used)
    return jnp.where(keep, tiled, 0.0).astype(BF16)


def _dot_nt(a, b):
    return lax.dot_general(a, b, (((1,), (1,)), ((), ())), preferred_element_type=F32)


def _head_diagonal(x, t):
    return jnp.concatenate(
        [x[h * t:(h + 1) * t, h * LANES:(h + 1) * LANES] for h in range(x.shape[1] // LANES)], axis=1)


def _sequence_copies(pt_ref, seq, n_pages, cache_hbm, buf, slot, sem):
    slots, n_head = cache_hbm.shape[1], cache_hbm.shape[2]
    copies = []
    for p in range(n_pages):
        page = pt_ref[seq * n_pages + p]
        for h in range(n_head):
            copies.append(pltpu.make_async_copy(
                cache_hbm.at[page, :, h, :], buf.at[slot, h, pl.ds(p * slots, slots), :], sem.at[slot]))
    return copies


def _stream_sequence(pt_ref, n_pages, caches_hbm, bufs, sems):
    b, n_seq = pl.program_id(0), pl.num_programs(0)
    slot = b % 2

    def each(seq, s, fn):
        for i, (cache, buf, sem) in enumerate(zip(caches_hbm, bufs, sems)):
            for j, c in enumerate(_sequence_copies(pt_ref, seq, n_pages, cache, buf, s, sem)):
                fn(c, (i + j) % 2)

    @pl.when(b == 0)
    def _():
        each(b, slot, lambda c, prio: c.start(priority=prio))

    @pl.when(b + 1 < n_seq)
    def _():
        each(b + 1, 1 - slot, lambda c, prio: c.start(priority=prio))

    each(b, slot, lambda c, prio: c.wait())
    return slot


def _load_block(buf, slot, b):
    rows = jnp.concatenate([buf[slot, h, b * MOBA_BLOCK:(b + 1) * MOBA_BLOCK, :] for h in range(buf.shape[1])],
                           axis=1)
    return rows.astype(BF16), rows


def _pad_rows(x, rows):
    return jnp.concatenate([x, jnp.zeros((rows - x.shape[0], x.shape[1]), x.dtype)], axis=0)


def _topk_bias_lanes(g, n_valid):
    idx = lax.broadcasted_iota(jnp.int32, g.shape, 1)
    rank = jnp.zeros(g.shape, jnp.int32)
    for m in range(n_valid):
        gm = g[:, m:m + 1]
        rank = rank + jnp.where(gm > g, 1, jnp.where((gm == g) & (idx > m), 1, 0))
    sel = (idx < n_valid) & (rank < MOBA_TOPK)
    return jnp.where(sel, 0.0, NEG_INF)


def _paged_attention(qbd, tq, kn_ref, vn_ref, kbuf, vbuf, slot, s_scr, block_bias):
    n_blk = kbuf.shape[2] // MOBA_BLOCK
    blk_max, blk_sum = [], []
    for b in range(n_blk):
        kb16, kb = _load_block(kbuf, slot, b)
        if block_bias is not None:
            blk_sum.append(jnp.sum(kb, axis=0, keepdims=True))
        s = _dot_nt(qbd, kb16)
        s_scr[:, b * MOBA_BLOCK:(b + 1) * MOBA_BLOCK] = s
        blk_max.append(jnp.max(s, axis=1, keepdims=True))
    s_new = _dot_nt(qbd, _pad_rows(kn_ref[...].astype(BF16), LANES))
    key_t = lax.broadcasted_iota(jnp.int32, s_new.shape, 1)
    qry_t = lax.broadcasted_iota(jnp.int32, s_new.shape, 0) % tq
    s_new = jnp.where(key_t <= qry_t, s_new, NEG_INF)
    bias = None
    if block_bias is not None:
        bias = block_bias(jnp.concatenate(blk_sum, axis=0) * (1.0 / MOBA_BLOCK))
    m = jnp.max(s_new, axis=1, keepdims=True)
    for b in range(n_blk):
        m = jnp.maximum(m, blk_max[b] if bias is None else blk_max[b] + bias[:, b:b + 1])
    p_new = jnp.exp2(s_new - m)
    l_new = jnp.sum(p_new, axis=1, keepdims=True)
    acc = jnp.dot(p_new.astype(BF16), _pad_rows(vn_ref[...].astype(BF16), LANES), preferred_element_type=F32)
    l_run = jnp.zeros((qbd.shape[0], MOBA_BLOCK), F32)
    for b in range(n_blk):
        s = s_scr[:, b * MOBA_BLOCK:(b + 1) * MOBA_BLOCK]
        if bias is not None:
            s = s + bias[:, b:b + 1]
        p = jnp.exp2(s - m)
        l_run = l_run + p
        acc = acc + jnp.dot(p.astype(BF16), _load_block(vbuf, slot, b)[0], preferred_element_type=F32)
    return acc / (l_new + jnp.sum(l_run, axis=1, keepdims=True))


def _sample_diff_kernel(n_pages, out_scale, pt_ref, lam_ref, q_ref, kn_ref, vn_ref, gsub_ref, ck_hbm, cv_hbm,
                        o_ref, kbuf, vbuf, ksem, vsem, s_scr):
    slot = _stream_sequence(pt_ref, n_pages, (ck_hbm, cv_hbm), (kbuf, vbuf), (ksem, vsem))
    tq = q_ref.shape[0]
    qbd = _block_diag_queries(q_ref[...].astype(F32), 2)
    acc = _paged_attention(qbd, tq, kn_ref, vn_ref, kbuf, vbuf, slot, s_scr, None)
    half = (q_ref.shape[1] // LANES) * tq
    o = _head_diagonal(acc[:half] - lam_ref[0, 0] * acc[half:2 * half], tq)
    g = gsub_ref[...]
    for h in range(o.shape[1] // LANES):
        cols = slice(h * LANES, (h + 1) * LANES)
        o_ref[:, cols] = (_rms(o[:, cols], g) * out_scale).astype(o_ref.dtype)


def _sample_moba_kernel(n_pages, pt_ref, q_ref, kn_ref, vn_ref, ck_hbm, cv_hbm,
                        o_ref, kbuf, vbuf, ksem, vsem, s_scr):
    slot = _stream_sequence(pt_ref, n_pages, (ck_hbm, cv_hbm), (kbuf, vbuf), (ksem, vsem))
    tq = q_ref.shape[0]
    qbd = _block_diag_queries(q_ref[...].astype(F32), 1)

    def block_bias(means):
        n_blk = means.shape[0]
        km_hi, km_lo = _split_hi_lo(_pad_rows(means, -(-n_blk // 16) * 16))
        return _topk_bias_lanes(_dot_nt(qbd, km_hi) + _dot_nt(qbd, km_lo), n_blk)

    acc = _paged_attention(qbd, tq, kn_ref, vn_ref, kbuf, vbuf, slot, s_scr, block_bias)
    o_ref[...] = _head_diagonal(acc, tq).astype(o_ref.dtype)


def _sample_attention(kernel_fn, name, page_table, scalars, q, k_new, v_new, extra, cache_k, cache_v, n_col):
    db, tq, w = q.shape
    n_pages = page_table.shape[1]
    n_head = w // LANES
    past = n_pages * cache_k.shape[1]
    tok = lambda: pl.BlockSpec((None, tq, w), lambda b, pt: (b, 0, 0))
    in_specs = ([pl.BlockSpec(memory_space=pltpu.SMEM) for _ in scalars] + [tok(), tok(), tok()]
                + [pl.BlockSpec((1, LANES), lambda b, pt: (0, 0)) for _ in extra]
                + [pl.BlockSpec(memory_space=pl.ANY)] * 2)
    seq_buf = pltpu.VMEM((2, n_head, past, LANES), F32)
    return pl.pallas_call(
        kernel_fn,
        grid_spec=pltpu.PrefetchScalarGridSpec(
            num_scalar_prefetch=1, grid=(db,), in_specs=in_specs, out_specs=tok(),
            scratch_shapes=[seq_buf, seq_buf, pltpu.SemaphoreType.DMA((2,)), pltpu.SemaphoreType.DMA((2,)),
                            pltpu.VMEM((n_col, past), F32)]),
        out_shape=jax.ShapeDtypeStruct((db, tq, w), BF16),
        compiler_params=_cparams("arbitrary"),
        name=name,
    )(page_table.reshape(-1), *scalars, q, k_new, v_new, *extra, cache_k, cache_v)


def _post_kernel(x_ref, oa_ref, ob_ref, pe_ref, gmix_ref, wg_ref, wa_ref, wb_ref, wo_ref, gmlp_ref,
                 wup_ref, wdn_ref, gple_ref, wpg_ref, wpp_ref, y_ref):
    dot = functools.partial(jnp.dot, preferred_element_type=F32)
    x = x_ref[...]
    d = x.shape[1]
    xn = _rms(x, gmix_ref[...]).astype(BF16)
    merged = (jax.nn.sigmoid(dot(xn, wg_ref[:, :d])) * dot(oa_ref[...], wa_ref[...])
              + jax.nn.sigmoid(dot(xn, wg_ref[:, d:])) * dot(ob_ref[...], wb_ref[...]))
    h = x + dot(merged.astype(BF16), wo_ref[...])
    u = dot(_rms(h, gmlp_ref[...]).astype(BF16), wup_ref[...])
    h = h + dot(jnp.square(jnp.maximum(u, 0.0)).astype(BF16), wdn_ref[...])
    gate = jax.nn.sigmoid(dot(_rms(h, gple_ref[...]).astype(BF16), wpg_ref[...]))
    y_ref[...] = h + gate * dot(pe_ref[...].astype(BF16), wpp_ref[...])


def _post(x, oa, ob, pe, g_mix, w_in16, weights, tm):
    m, d = x.shape
    row = lambda a: pl.BlockSpec((tm, a.shape[1]), lambda i: (i, 0))
    whole = lambda a: pl.BlockSpec(a.shape, lambda i: (0, 0), pipeline_mode=pl.Buffered(1))
    gate_w = pl.BlockSpec((d, 2 * d), lambda i: (0, N_QKV_SEG // 2), pipeline_mode=pl.Buffered(1))
    acts = (x, oa, ob, pe)
    return pl.pallas_call(
        _post_kernel, grid=(m // tm,),
        in_specs=[row(a) for a in acts] + [whole(g_mix), gate_w] + [whole(a) for a in weights],
        out_specs=pl.BlockSpec((tm, d), lambda i: (i, 0)),
        out_shape=jax.ShapeDtypeStruct((m, d), F32),
        compiler_params=_cparams("parallel"),
        name="post",
    )(*acts, g_mix, w_in16, *weights)


def _row_tile(m, cap):
    t = min(m, cap)
    assert m % t == 0, (m, t)
    return t


def kernel(x_prompt, x_sample, p_prompt, p_sample, cache_a_k, cache_a_v, cache_b_k, cache_b_v, page_table,
           g_mix, w_in, qn_a, kn_a, lam_q1, lam_k1, lam_q2, lam_k2, g_sub_a, qn_b, kn_b,
           w_br_a, w_br_b, w_o, g_mlp, w_up, w_down, g_ple, w_ple_gate, w_ple_proj):
    b, s, d = x_prompt.shape
    db, t, _ = x_sample.shape
    depth = g_mix.shape[0]
    n_pool, page = cache_a_k.shape[1], cache_a_k.shape[2]
    n_pages = page_table.shape[1]
    past = n_pages * page
    w_a, w_b = H_A * 2 * D_A, H_B * D_B
    assert s % ATT_TILE == 0 and ATT_TILE == MOBA_BLOCK and w_a == w_b == d and H_A == H_B
    assert past % MOBA_BLOCK == 0 and MOBA_BLOCK % page == 0 and t <= MOBA_BLOCK and past // MOBA_BLOCK >= MOBA_TOPK
    assert w_in.shape[2] == N_SEG * d

    pos_p = jnp.arange(s, dtype=jnp.int32)
    pos_s = past + jnp.arange(t, dtype=jnp.int32)
    tm_p = _row_tile(s, PROJ_ROWS)
    tm_s = _row_tile(db * t, PROJ_ROWS)
    assert tm_s % t == 0
    tab_p = _rope_tables(pos_p, D_A) + _rope_tables(pos_p, D_B)
    tab_s = tuple(jnp.tile(a, (tm_s // t, 1)) for dd in (D_A, D_B) for a in _rope_tables(pos_s, dd))

    hp = x_prompt.reshape(b * s, d)
    hs = x_sample.reshape(db * t, d)
    caches = ([], [], [], [], [], [], [], [])
    tile2 = lambda g: jnp.tile(g, 2)[None, :]
    for l in range(depth):
        lam_init = 0.8 - 0.6 * math.exp(-0.3 * l)
        lam = (jnp.exp(jnp.sum(lam_q1[l] * lam_k1[l])) - jnp.exp(jnp.sum(lam_q2[l] * lam_k2[l])) + lam_init)
        lam = lam.reshape(1, 1).astype(F32)
        out_scale = 1.0 - lam_init
        g_mix_l = g_mix[l][None, :]
        w_in16 = w_in[l].astype(BF16)
        gains = (tile2(qn_a[l]) * (D_A ** -0.5 * LOG2_E), tile2(kn_a[l]),
                 qn_b[l][None, :] * (D_B ** -0.5 * LOG2_E), kn_b[l][None, :])
        post_w = (w_br_a[l].astype(BF16), w_br_b[l].astype(BF16), w_o[l].astype(BF16), g_mlp[l][None, :],
                  w_up[l].astype(BF16), w_down[l].astype(BF16), g_ple[l][None, :],
                  w_ple_gate[l].astype(BF16), w_ple_proj[l].astype(BF16))
        g_sub = g_sub_a[l][None, :]

        qa, ka, va, qb, kb, vb, ka16, va16, kb16, vb16 = _project(hp, g_mix_l, w_in16, gains, tab_p, tm_p)
        r3 = lambda a: a.reshape(b, s, d)
        oa = _diff_attention(lam, r3(qa), r3(ka16), r3(va16), g_sub, out_scale).reshape(b * s, d)
        ob = _moba_attention(r3(qb), r3(kb16), r3(vb16)).reshape(b * s, d)
        hp = _post(hp, oa, ob, p_prompt[l].reshape(b * s, -1), g_mix_l, w_in16, post_w,
                   _row_tile(b * s, POST_ROWS))
        for dst, a in zip(caches[:4], (ka, va, kb, vb)):
            dst.append(a.reshape(b, s, H_A, LANES))

        qa, ka, va, qb, kb, vb = _project(hs, g_mix_l, w_in16, gains, tab_s, tm_s)[:6]
        r3 = lambda a: a.reshape(db, t, d)
        c3 = lambda c: c[l]
        n_col = -(-2 * H_A * t // LANES) * LANES
        oa = _sample_attention(functools.partial(_sample_diff_kernel, n_pages, out_scale), "sample_diff_attn",
                               page_table, (lam,), r3(qa), r3(ka), r3(va), (g_sub,),
                               c3(cache_a_k), c3(cache_a_v), n_col)
        n_col = -(-H_B * t // LANES) * LANES
        ob = _sample_attention(functools.partial(_sample_moba_kernel, n_pages), "sample_moba_attn",
                               page_table, (), r3(qb), r3(kb), r3(vb), (), c3(cache_b_k), c3(cache_b_v), n_col)
        hs = _post(hs, oa.reshape(db * t, d), ob.reshape(db * t, d), p_sample[l].reshape(db * t, -1),
                   g_mix_l, w_in16, post_w, _row_tile(db * t, POST_ROWS))
        for dst, a in zip(caches[4:], (ka, va, kb, vb)):
            dst.append(a.reshape(db, t, H_A, LANES))

    return (hp.reshape(b, s, d), hs.reshape(db, t, d)) + tuple(jnp.stack(c) for c in caches)
```
